```python
import jax
import jax.numpy as jnp
from jax import lax
import numpy as np

D_MODEL = 4096
BATCH = 4
SEQ = 2048
DEPTH = 2
DEC_BATCH = 8
DEC_SEQ = 4
PAST_LEN = 16384
PAGE_SIZE = 128

N_MIXERS = 2
N_MLSTM_LAYERS = (DEPTH + 1) // N_MIXERS
N_FOX_LAYERS = DEPTH // N_MIXERS
MLSTM_HEADS = 8
MLSTM_DQK = D_MODEL // (2 * MLSTM_HEADS)
MLSTM_DV = D_MODEL // MLSTM_HEADS
MLSTM_CHUNK = 64
MLSTM_SPLITS = [MLSTM_HEADS * MLSTM_DQK, 2 * MLSTM_HEADS * MLSTM_DQK, 2 * MLSTM_HEADS * MLSTM_DQK + MLSTM_HEADS * MLSTM_DV, 2 * MLSTM_HEADS * MLSTM_DQK + 2 * MLSTM_HEADS * MLSTM_DV]
MLSTM_IN = MLSTM_SPLITS[-1] + 2 * MLSTM_HEADS
FOX_HEAD_DIM = 128
FOX_HEADS = D_MODEL // FOX_HEAD_DIM
FOX_WIDTH = FOX_HEADS * FOX_HEAD_DIM
FOX_IN = 3 * FOX_WIDTH + FOX_HEADS
Q_BLOCK = 128
MEM_TOKENS = 256
XATTN_HEADS = 4
XATTN_HEAD_DIM = 128
XATTN_WIDTH = XATTN_HEADS * XATTN_HEAD_DIM
D_FF = 256 * ((8 * D_MODEL // 3 + 255) // 256)
FFN_RES_SCALE = 0.5
RMS_EPS = 1e-6

kernel_name = 'mlstm_fox_macaron_memxattn_step'


def rms_norm(x, g):
    xf = x.astype(jnp.float32)
    y = xf * lax.rsqrt(jnp.mean(xf * xf, axis=-1, keepdims=True) + RMS_EPS)
    return (y * g.astype(jnp.float32)).astype(x.dtype)


def swiglu_ffn(x, w_in, w_out):
    gate, up = jnp.split(x @ w_in, 2, axis=-1)
    return (jax.nn.silu(gate) * up) @ w_out


def mlstm_chunk_step(carry, inp):
    c_prev, n_prev, m_prev = carry
    q, k, v, ig, lf = inp
    L = q.shape[2]
    b = jnp.cumsum(lf, axis=-1)
    causal = jnp.tril(jnp.ones((L, L), dtype=bool))
    d = jnp.where(causal, b[..., :, None] - b[..., None, :] + ig[..., None, :], -jnp.inf)
    inter = b + m_prev[..., None]
    m_t = jnp.maximum(jnp.max(d, axis=-1), inter)
    s = jnp.einsum('bhtd,bhsd->bhts', q, k) * jnp.exp(d - m_t[..., None])
    w_inter = jnp.exp(inter - m_t)
    num = w_inter[..., None] * jnp.einsum('bhtd,bhdv->bhtv', q, c_prev) + jnp.einsum('bhts,bhsv->bhtv', s, v)
    den = w_inter * jnp.einsum('bhtd,bhd->bht', q, n_prev) + jnp.sum(s, axis=-1)
    h = num / jnp.maximum(jnp.abs(den), jnp.exp(-m_t))[..., None]
    m_new = m_t[..., -1]
    w_c = jnp.exp(b[..., -1] + m_prev - m_new)
    w_s = jnp.exp(b[..., -1:] - b + ig - m_new[..., None])
    c_new = w_c[..., None, None] * c_prev + jnp.einsum('bhsd,bhsv->bhdv', k * w_s[..., None], v)
    n_new = w_c[..., None] * n_prev + jnp.einsum('bhs,bhsd->bhd', w_s, k)
    return (c_new, n_new, m_new), h


def mlstm_mixer(x, c0, n0, m0, w_in, b_gate, out_norm, w_out):
    B, T, _ = x.shape
    H = MLSTM_HEADS
    q, k, v, o, gates = jnp.split(x @ w_in, MLSTM_SPLITS, axis=-1)

    def heads(a, dim):
        return a.reshape(B, T, H, dim).transpose(0, 2, 1, 3).astype(jnp.float32)

    q = heads(q, MLSTM_DQK)
    k = heads(k, MLSTM_DQK) * (MLSTM_DQK ** -0.5)
    v = heads(v, MLSTM_DV)
    gates = (gates.astype(jnp.float32) + b_gate.astype(jnp.float32)).transpose(0, 2, 1)
    ig = gates[:, :H]
    lf = jax.nn.log_sigmoid(gates[:, H:])
    L = MLSTM_CHUNK if T % MLSTM_CHUNK == 0 else T
    nc = T // L

    def chunks(a):
        return jnp.moveaxis(a.reshape(B, H, nc, L, *a.shape[3:]), 2, 0)

    carry0 = (c0.astype(jnp.float32), n0.astype(jnp.float32), m0.astype(jnp.float32))
    (c, n, m), h = lax.scan(mlstm_chunk_step, carry0, (chunks(q), chunks(k), chunks(v), chunks(ig), chunks(lf)))
    h = jnp.moveaxis(h, 0, 2).reshape(B, H, T, MLSTM_DV).transpose(0, 2, 1, 3)
    h = rms_norm(h, out_norm).reshape(B, T, H * MLSTM_DV).astype(x.dtype)
    y = (h * jax.nn.sigmoid(o)) @ w_out
    return y, c, n, m


def fox_project(x, w_in, b_f, q_norm, k_norm):
    B, T, _ = x.shape
    q, k, v, f = jnp.split(x @ w_in, [FOX_WIDTH, 2 * FOX_WIDTH, 3 * FOX_WIDTH], axis=-1)
    shape = (B, T, FOX_HEADS, FOX_HEAD_DIM)
    q = rms_norm(q.reshape(shape), q_norm) * (FOX_HEAD_DIM ** -0.5)
    k = rms_norm(k.reshape(shape), k_norm)
    v = v.reshape(shape)
    lf = jax.nn.log_sigmoid(f.astype(jnp.float32) + b_f.astype(jnp.float32))
    return q, k, v, lf


def fox_prompt_attention(q, k, v, lf):
    B, T, H, D = q.shape
    c = jnp.cumsum(lf, axis=1)
    kb = -c.transpose(0, 2, 1)
    kpos = jnp.arange(T)
    vf = v.astype(jnp.float32)
    blk = Q_BLOCK if T % Q_BLOCK == 0 else T
    nb = T // blk

    def block(args):
        qb, cb, qpos = args
        s = jnp.einsum('bqhd,bkhd->bhqk', qb, k, preferred_element_type=jnp.float32)
        s = s + cb.transpose(0, 2, 1)[..., None] + kb[:, :, None, :]
        s = jnp.where(kpos[None, :] <= qpos[:, None], s, -jnp.inf)
        p = jax.nn.softmax(s, axis=-1)
        return jnp.einsum('bhqk,bkhd->bqhd', p, vf)

    qs = jnp.moveaxis(q.reshape(B, nb, blk, H, D), 1, 0)
    cs = jnp.moveaxis(c.reshape(B, nb, blk, H), 1, 0)
    ps = jnp.arange(T).reshape(nb, blk)
    o = lax.map(block, (qs, cs, ps))
    return jnp.moveaxis(o, 0, 1).reshape(B, T, H, D)


def fox_mixer_prompt(x, w_in, b_f, q_norm, k_norm, w_out):
    B, T, _ = x.shape
    q, k, v, lf = fox_project(x, w_in, b_f, q_norm, k_norm)
    o = fox_prompt_attention(q, k, v, lf).astype(x.dtype)
    y = o.reshape(B, T, FOX_WIDTH) @ w_out

    def pages(a):
        return a.reshape(B, T // PAGE_SIZE, PAGE_SIZE, *a.shape[2:])

    return y, pages(k), pages(v), pages(lf)


def fox_mixer_decode(x, pool_k, pool_v, pool_lf, j, page_table, w_in, b_f, q_norm, k_norm, w_out):
    B, T, _ = x.shape
    q, k, v, lf = fox_project(x, w_in, b_f, q_norm, k_norm)
    P = page_table.shape[1] * PAGE_SIZE
    k_past = pool_k[j, page_table].reshape(B, P, FOX_HEADS, FOX_HEAD_DIM)
    v_past = pool_v[j, page_table].reshape(B, P, FOX_HEADS, FOX_HEAD_DIM)
    lf_past = pool_lf[j, page_table].reshape(B, P, FOX_HEADS).astype(jnp.float32)
    r = lax.cumsum(lf_past, axis=1, reverse=True) - lf_past
    a = jnp.cumsum(lf, axis=1)
    a_q = a.transpose(0, 2, 1)[..., None]
    s_past = jnp.einsum('bqhd,bkhd->bhqk', q, k_past, preferred_element_type=jnp.float32) + a_q + r.transpose(0, 2, 1)[:, :, None, :]
    s_new = jnp.einsum('bqhd,bkhd->bhqk', q, k, preferred_element_type=jnp.float32) + a_q - a.transpose(0, 2, 1)[:, :, None, :]
    s_new = jnp.where(jnp.tril(jnp.ones((T, T), dtype=bool)), s_new, -jnp.inf)
    p = jax.nn.softmax(jnp.concatenate([s_past, s_new], axis=-1), axis=-1)
    o = jnp.einsum('bhqk,bkhd->bqhd', p[..., :P], v_past) + jnp.einsum('bhqk,bkhd->bqhd', p[..., P:], v)
    y = o.astype(x.dtype).reshape(B, T, FOX_WIDTH) @ w_out
    return y, k, v, lf


def mem_kv(mem, norm_m, w_kv, k_norm):
    B, M, _ = mem.shape
    k, v = jnp.split(rms_norm(mem, norm_m) @ w_kv, 2, axis=-1)
    k = rms_norm(k.reshape(B, M, XATTN_HEADS, XATTN_HEAD_DIM), k_norm)
    v = v.reshape(B, M, XATTN_HEADS, XATTN_HEAD_DIM)
    return k, v


def mem_cross_attention(x, mk, mv, w_q, q_norm, w_o):
    B, T, _ = x.shape
    q = rms_norm((x @ w_q).reshape(B, T, XATTN_HEADS, XATTN_HEAD_DIM), q_norm) * (XATTN_HEAD_DIM ** -0.5)
    s = jnp.einsum('bqhd,bmhd->bhqm', q, mk, preferred_element_type=jnp.float32)
    p = jax.nn.softmax(s, axis=-1)
    o = jnp.einsum('bhqm,bmhd->bqhd', p, mv).astype(x.dtype)
    return o.reshape(B, T, XATTN_WIDTH) @ w_o


def setup_inputs(seed: int = 0) -> dict:
    key = jax.random.key(seed)
    keys = jax.random.split(key, 40)
    f32 = jnp.float32

    def nrm(i, shape, scale):
        return scale * jax.random.normal(keys[i], shape, f32)

    n_pages = PAST_LEN // PAGE_SIZE
    n_used = DEC_BATCH * n_pages
    n_pool = n_used + max(1, n_used // 4)
    page_table = jax.random.permutation(keys[0], n_pool)[:n_used].reshape(DEC_BATCH, n_pages).astype(jnp.int32)
    fox_f_base = jnp.linspace(1.0, 5.0, FOX_HEADS, dtype=f32)
    mlstm_f_base = jnp.linspace(3.0, 6.0, MLSTM_HEADS, dtype=f32)
    hv = MLSTM_HEADS * MLSTM_DV
    return {
        'x_prompt': nrm(1, (BATCH, SEQ, D_MODEL), 1.0),
        'x_sample': nrm(2, (DEC_BATCH, DEC_SEQ, D_MODEL), 1.0),
        'state_mlstm_C': nrm(3, (N_MLSTM_LAYERS, DEC_BATCH, MLSTM_HEADS, MLSTM_DQK, MLSTM_DV), 0.05),
        'state_mlstm_n': nrm(4, (N_MLSTM_LAYERS, DEC_BATCH, MLSTM_HEADS, MLSTM_DQK), 0.5),
        'state_mlstm_m': nrm(5, (N_MLSTM_LAYERS, DEC_BATCH, MLSTM_HEADS), 1.0),
        'cache_fox_k': nrm(6, (N_FOX_LAYERS, n_pool, PAGE_SIZE, FOX_HEADS, FOX_HEAD_DIM), 1.0),
        'cache_fox_v': nrm(7, (N_FOX_LAYERS, n_pool, PAGE_SIZE, FOX_HEADS, FOX_HEAD_DIM), 1.0),
        'cache_fox_logf': jax.nn.log_sigmoid(nrm(8, (N_FOX_LAYERS, n_pool, PAGE_SIZE, FOX_HEADS), 1.0) + fox_f_base),
        'cache_mem_k': nrm(9, (DEPTH, DEC_BATCH, MEM_TOKENS, XATTN_HEADS, XATTN_HEAD_DIM), 1.0),
        'cache_mem_v': nrm(10, (DEPTH, DEC_BATCH, MEM_TOKENS, XATTN_HEADS, XATTN_HEAD_DIM), 1.0),
        'page_table': page_table,
        'mem_prompt': nrm(11, (BATCH, MEM_TOKENS, D_MODEL), 1.0),
        'norm_ffn1': 1.0 + nrm(12, (DEPTH, D_MODEL), 0.02),
        'w_ffn1_in': nrm(13, (DEPTH, D_MODEL, 2 * D_FF), D_MODEL ** -0.5),
        'w_ffn1_out': nrm(14, (DEPTH, D_FF, D_MODEL), D_FF ** -0.5),
        'norm_mix': 1.0 + nrm(15, (DEPTH, D_MODEL), 0.02),
        'w_mlstm_in': nrm(16, (N_MLSTM_LAYERS, D_MODEL, MLSTM_IN), D_MODEL ** -0.5),
        'b_mlstm_gate': jnp.concatenate([nrm(17, (N_MLSTM_LAYERS, MLSTM_HEADS), 0.1), mlstm_f_base + nrm(18, (N_MLSTM_LAYERS, MLSTM_HEADS), 0.1)], axis=-1),
        'mlstm_out_norm': 1.0 + nrm(19, (N_MLSTM_LAYERS, MLSTM_HEADS, MLSTM_DV), 0.02),
        'w_mlstm_out': nrm(20, (N_MLSTM_LAYERS, hv, D_MODEL), hv ** -0.5),
        'w_fox_in': nrm(21, (N_FOX_LAYERS, D_MODEL, FOX_IN), D_MODEL ** -0.5),
        'b_fox_f': fox_f_base + nrm(22, (N_FOX_LAYERS, FOX_HEADS), 0.1),
        'fox_q_norm': 1.0 + nrm(23, (N_FOX_LAYERS, FOX_HEAD_DIM), 0.02),
        'fox_k_norm': 1.0 + nrm(24, (N_FOX_LAYERS, FOX_HEAD_DIM), 0.02),
        'w_fox_out': nrm(25, (N_FOX_LAYERS, FOX_WIDTH, D_MODEL), FOX_WIDTH ** -0.5),
        'norm_xattn': 1.0 + nrm(26, (DEPTH, D_MODEL), 0.02),
        'norm_mem': 1.0 + nrm(27, (DEPTH, D_MODEL), 0.02),
        'w_xq': nrm(28, (DEPTH, D_MODEL, XATTN_WIDTH), D_MODEL ** -0.5),
        'w_xkv': nrm(29, (DEPTH, D_MODEL, 2 * XATTN_WIDTH), D_MODEL ** -0.5),
        'xq_norm': 1.0 + nrm(30, (DEPTH, XATTN_HEAD_DIM), 0.02),
        'xk_norm': 1.0 + nrm(31, (DEPTH, XATTN_HEAD_DIM), 0.02),
        'w_xo': nrm(32, (DEPTH, XATTN_WIDTH, D_MODEL), XATTN_WIDTH ** -0.5),
        'norm_ffn2': 1.0 + nrm(33, (DEPTH, D_MODEL), 0.02),
        'w_ffn2_in': nrm(34, (DEPTH, D_MODEL, 2 * D_FF), D_MODEL ** -0.5),
        'w_ffn2_out': nrm(35, (DEPTH, D_FF, D_MODEL), D_FF ** -0.5),
    }


def reference(x_prompt, x_sample, state_mlstm_C, state_mlstm_n, state_mlstm_m, cache_fox_k, cache_fox_v, cache_fox_logf, cache_mem_k, cache_mem_v, page_table, mem_prompt, norm_ffn1, w_ffn1_in, w_ffn1_out, norm_mix, w_mlstm_in, b_mlstm_gate, mlstm_out_norm, w_mlstm_out, w_fox_in, b_fox_f, fox_q_norm, fox_k_norm, w_fox_out, norm_xattn, norm_mem, w_xq, w_xkv, xq_norm, xk_norm, w_xo, norm_ffn2, w_ffn2_in, w_ffn2_out):
    yp, ys = x_prompt, x_sample
    bp = x_prompt.shape[0]
    c_p, n_p, m_p, c_s, n_s, m_s = [], [], [], [], [], []
    fk_p, fv_p, fl_p, fk_s, fv_s, fl_s = [], [], [], [], [], []
    mk_p, mv_p = [], []
    for layer in range(DEPTH):
        j = layer // N_MIXERS
        yp = yp + FFN_RES_SCALE * swiglu_ffn(rms_norm(yp, norm_ffn1[layer]), w_ffn1_in[layer], w_ffn1_out[layer])
        ys = ys + FFN_RES_SCALE * swiglu_ffn(rms_norm(ys, norm_ffn1[layer]), w_ffn1_in[layer], w_ffn1_out[layer])
        hp = rms_norm(yp, norm_mix[layer])
        hs = rms_norm(ys, norm_mix[layer])
        if layer % N_MIXERS == 0:
            zc = jnp.zeros((bp, MLSTM_HEADS, MLSTM_DQK, MLSTM_DV), jnp.float32)
            zn = jnp.zeros((bp, MLSTM_HEADS, MLSTM_DQK), jnp.float32)
            zm = jnp.zeros((bp, MLSTM_HEADS), jnp.float32)
            dp, c1, n1, m1 = mlstm_mixer(hp, zc, zn, zm, w_mlstm_in[j], b_mlstm_gate[j], mlstm_out_norm[j], w_mlstm_out[j])
            ds, c2, n2, m2 = mlstm_mixer(hs, state_mlstm_C[j], state_mlstm_n[j], state_mlstm_m[j], w_mlstm_in[j], b_mlstm_gate[j], mlstm_out_norm[j], w_mlstm_out[j])
            c_p.append(c1); n_p.append(n1); m_p.append(m1)
            c_s.append(c2); n_s.append(n2); m_s.append(m2)
        else:
            dp, k1, v1, l1 = fox_mixer_prompt(hp, w_fox_in[j], b_fox_f[j], fox_q_norm[j], fox_k_norm[j], w_fox_out[j])
            ds, k2, v2, l2 = fox_mixer_decode(hs, cache_fox_k, cache_fox_v, cache_fox_logf, j, page_table, w_fox_in[j], b_fox_f[j], fox_q_norm[j], fox_k_norm[j], w_fox_out[j])
            fk_p.append(k1); fv_p.append(v1); fl_p.append(l1)
            fk_s.append(k2); fv_s.append(v2); fl_s.append(l2)
        yp = yp + dp
        ys = ys + ds
        mk, mv = mem_kv(mem_prompt, norm_mem[layer], w_xkv[layer], xk_norm[layer])
        mk_p.append(mk); mv_p.append(mv)
        yp = yp + mem_cross_attention(rms_norm(yp, norm_xattn[layer]), mk, mv, w_xq[layer], xq_norm[layer], w_xo[layer])
        ys = ys + mem_cross_attention(rms_norm(ys, norm_xattn[layer]), cache_mem_k[layer], cache_mem_v[layer], w_xq[layer], xq_norm[layer], w_xo[layer])
        yp = yp + FFN_RES_SCALE * swiglu_ffn(rms_norm(yp, norm_ffn2[layer]), w_ffn2_in[layer], w_ffn2_out[layer])
        ys = ys + FFN_RES_SCALE * swiglu_ffn(rms_norm(ys, norm_ffn2[layer]), w_ffn2_in[layer], w_ffn2_out[layer])
    return (yp, ys, jnp.stack(c_p), jnp.stack(n_p), jnp.stack(m_p), jnp.stack(fk_p), jnp.stack(fv_p), jnp.stack(fl_p), jnp.stack(mk_p), jnp.stack(mv_p), jnp.stack(c_s), jnp.stack(n_s), jnp.stack(m_s), jnp.stack(fk_s), jnp.stack(fv_s), jnp.stack(fl_s))
```

```python
import functools

import jax
import jax.numpy as jnp
from jax import lax
from jax.experimental import pallas as pl
from jax.experimental.pallas import tpu as pltpu

RMS_EPS = 1e-6
FFN_RES_SCALE = 0.5
LANES = 128
SAMPLE_ROWS = 16
VMEM_LIMIT_BYTES = 56 * 1024 * 1024
MLSTM_PROMPT_CHUNK = 256
FOX_Q_BLOCK = 256

f32 = jnp.float32
bf16 = jnp.bfloat16


def _cparams(n_axes):
    return pltpu.CompilerParams(
        dimension_semantics=("arbitrary",) * n_axes,
        vmem_limit_bytes=VMEM_LIMIT_BYTES)


def _dot(a, b):
    return jnp.dot(a, b, preferred_element_type=f32)


def _dot_nt(a, b):
    return lax.dot_general(a, b, (((1,), (1,)), ((), ())), preferred_element_type=f32)


def _dot_tn(a, b):
    return lax.dot_general(a, b, (((0,), (0,)), ((), ())), preferred_element_type=f32)


def _sigmoid(x):
    return 1.0 / (1.0 + jnp.exp(-x))


def _log_sigmoid(x):
    return -(jnp.maximum(-x, 0.0) + jnp.log(1.0 + jnp.exp(-jnp.abs(x))))


def _pick(n, candidates):
    for c in candidates:
        if n % c == 0:
            return c
    return n


def _rmsnorm_kernel(x_ref, g_ref, o_ref):
    x = x_ref[...]
    ms = jnp.mean(x * x, axis=-1, keepdims=True)
    o_ref[...] = (x * lax.rsqrt(ms + RMS_EPS) * g_ref[...]).astype(o_ref.dtype)


def rmsnorm(x, g):
    m, d = x.shape
    bm = _pick(m, (256, 128))
    return pl.pallas_call(
        _rmsnorm_kernel,
        grid=(m // bm,),
        in_specs=[pl.BlockSpec((bm, d), lambda i: (i, 0)),
                  pl.BlockSpec((1, d), lambda i: (0, 0))],
        out_specs=pl.BlockSpec((bm, d), lambda i: (i, 0)),
        out_shape=jax.ShapeDtypeStruct((m, d), bf16),
        compiler_params=_cparams(1),
        name="rmsnorm",
    )(x, g.reshape(1, d).astype(f32))


def _mm_plain_kernel(x_ref, w_ref, o_ref):
    o_ref[...] = _dot(x_ref[...], w_ref[...]).astype(o_ref.dtype)


def _mm_residual_kernel(x_ref, w_ref, r_ref, o_ref, *, scale):
    acc = _dot(x_ref[...], w_ref[...])
    if scale != 1.0:
        acc = scale * acc
    o_ref[...] = r_ref[...] + acc


def _mm_swiglu_kernel(x_ref, wg_ref, wu_ref, o_ref):
    x = x_ref[...]
    g = _dot(x, wg_ref[...])
    u = _dot(x, wu_ref[...])
    o_ref[...] = (g * _sigmoid(g) * u).astype(o_ref.dtype)


def _mm_headnorm_kernel(x_ref, w_ref, g_ref, o_ref, *, scale, head_dim):
    acc = _dot(x_ref[...], w_ref[...])
    g = g_ref[...]
    for s in range(0, acc.shape[1], head_dim):
        a = acc[:, s:s + head_dim]
        ms = jnp.mean(a * a, axis=-1, keepdims=True)
        y = a * lax.rsqrt(ms + RMS_EPS) * g
        if scale != 1.0:
            y = y * scale
        o_ref[:, s:s + head_dim] = y.astype(o_ref.dtype)


def _mm_logsigmoid_kernel(x_ref, w_ref, b_ref, o_ref):
    o_ref[...] = _log_sigmoid(_dot(x_ref[...], w_ref[...]) + b_ref[...])


def _mm_blocks(m, n, bm, bn):
    bm = bm or _pick(m, (1024, 512, 256, 128))
    bn = bn or _pick(n, (512, 256, 128))
    assert m % bm == 0 and n % bn == 0, (m, n, bm, bn)
    return bm, bn


def matmul(x, w, *, col0=0, n=None, out_dtype=f32, bm=None, bn=None):
    m, k = x.shape
    n = n or w.shape[1]
    bm, bn = _mm_blocks(m, n, bm, bn)
    assert col0 % bn == 0
    c0 = col0 // bn
    return pl.pallas_call(
        _mm_plain_kernel,
        grid=(m // bm, n // bn),
        in_specs=[pl.BlockSpec((bm, k), lambda i, j: (i, 0)),
                  pl.BlockSpec((k, bn), lambda i, j: (0, j + c0))],
        out_specs=pl.BlockSpec((bm, bn), lambda i, j: (i, j)),
        out_shape=jax.ShapeDtypeStruct((m, n), out_dtype),
        compiler_params=_cparams(2),
        name="matmul",
    )(x, w)


def matmul_residual(x, w, res, scale, *, bm=None, bn=None):
    m, k = x.shape
    n = w.shape[1]
    bm, bn = _mm_blocks(m, n, bm, bn)
    return pl.pallas_call(
        functools.partial(_mm_residual_kernel, scale=scale),
        grid=(m // bm, n // bn),
        in_specs=[pl.BlockSpec((bm, k), lambda i, j: (i, 0)),
                  pl.BlockSpec((k, bn), lambda i, j: (0, j)),
                  pl.BlockSpec((bm, bn), lambda i, j: (i, j))],
        out_specs=pl.BlockSpec((bm, bn), lambda i, j: (i, j)),
        out_shape=jax.ShapeDtypeStruct((m, n), f32),
        compiler_params=_cparams(2),
        name="matmul_residual",
    )(x, w, res)


def matmul_swiglu(x, w_in, *, bm=None, bn=None):
    m, k = x.shape
    f = w_in.shape[1] // 2
    bm, bn = _mm_blocks(m, f, bm, bn or _pick(f, (256, 128)))
    nf = f // bn
    return pl.pallas_call(
        _mm_swiglu_kernel,
        grid=(m // bm, nf),
        in_specs=[pl.BlockSpec((bm, k), lambda i, j: (i, 0)),
                  pl.BlockSpec((k, bn), lambda i, j: (0, j)),
                  pl.BlockSpec((k, bn), lambda i, j: (0, j + nf))],
        out_specs=pl.BlockSpec((bm, bn), lambda i, j: (i, j)),
        out_shape=jax.ShapeDtypeStruct((m, f), bf16),
        compiler_params=_cparams(2),
        name="matmul_swiglu",
    )(x, w_in, w_in)


def matmul_headnorm(x, w, g, *, col0=0, n=None, scale=1.0, out_dtype=f32, bm=None, bn=None):
    m, k = x.shape
    n = n or w.shape[1]
    hd = g.shape[-1]
    bm, bn = _mm_blocks(m, n, bm, bn)
    assert col0 % bn == 0 and bn % hd == 0
    c0 = col0 // bn
    return pl.pallas_call(
        functools.partial(_mm_headnorm_kernel, scale=scale, head_dim=hd),
        grid=(m // bm, n // bn),
        in_specs=[pl.BlockSpec((bm, k), lambda i, j: (i, 0)),
                  pl.BlockSpec((k, bn), lambda i, j: (0, j + c0)),
                  pl.BlockSpec((1, hd), lambda i, j: (0, 0))],
        out_specs=pl.BlockSpec((bm, bn), lambda i, j: (i, j)),
        out_shape=jax.ShapeDtypeStruct((m, n), out_dtype),
        compiler_params=_cparams(2),
        name="matmul_headnorm",
    )(x, w, g.reshape(1, hd).astype(f32))


def matmul_logsigmoid(x, w, b, *, bm=None):
    m, k = x.shape
    n = w.shape[1]
    bm = bm or _pick(m, (1024, 512, 256, 128))
    return pl.pallas_call(
        _mm_logsigmoid_kernel,
        grid=(m // bm,),
        in_specs=[pl.BlockSpec((bm, k), lambda i: (i, 0)),
                  pl.BlockSpec((k, n), lambda i: (0, 0)),
                  pl.BlockSpec((1, n), lambda i: (0, 0))],
        out_specs=pl.BlockSpec((bm, n), lambda i: (i, 0)),
        out_shape=jax.ShapeDtypeStruct((m, n), f32),
        compiler_params=_cparams(1),
        name="matmul_logsigmoid",
    )(x, w, b.reshape(1, n).astype(f32))


def _mlstm_kernel(q_ref, k_ref, v_ref, o_ref, gates_ref, bias_ref, onorm_ref,
                  c0_ref, n0_ref, m0_ref,
                  y_ref, c_ref, n_ref, m_ref, *, heads, t_valid):
    h = pl.program_id(1)
    ci = pl.program_id(2)
    L, dqk = q_ref.shape

    @pl.when(ci == 0)
    def _():
        c_ref[...] = c0_ref[...]
        n_ref[...] = n0_ref[...]
        m_ref[...] = m0_ref[...]

    row = lax.broadcasted_iota(jnp.int32, (L, L), 0)
    col = lax.broadcasted_iota(jnp.int32, (L, L), 1)
    eye = row == col
    causal = col <= row

    gates = gates_ref[...] + bias_ref[...]
    lane = lax.broadcasted_iota(jnp.int32, gates.shape, 1)
    ig_col = jnp.sum(jnp.where(lane == h, gates, 0.0), axis=1, keepdims=True)
    f_col = jnp.sum(jnp.where(lane == h + heads, gates, 0.0), axis=1, keepdims=True)
    lf_col = _log_sigmoid(f_col)
    if t_valid is not None:
        rvalid = lax.broadcasted_iota(jnp.int32, (L, 1), 0) < t_valid
        lf_col = jnp.where(rvalid, lf_col, 0.0)
        ig_col = jnp.where(rvalid, ig_col, -jnp.inf)

    b_row = jnp.sum(jnp.where(row <= col, lf_col, 0.0), axis=0, keepdims=True)
    lf_row = jnp.sum(jnp.where(eye, lf_col, 0.0), axis=0, keepdims=True)
    b_col = jnp.sum(jnp.where(causal, lf_row, 0.0), axis=1, keepdims=True)
    ig_row = jnp.sum(jnp.where(eye, ig_col, 0.0), axis=0, keepdims=True)
    u_row = ig_row - b_row
    u_col = ig_col - b_col

    m_prev = m_ref[:, 0:1]
    d = jnp.where(causal, b_col + u_row, -jnp.inf)
    inter = b_col + m_prev
    m_t = jnp.maximum(jnp.max(d, axis=1, keepdims=True), inter)
    p = jnp.exp(d - m_t)
    w_inter = jnp.exp(inter - m_t)

    q = q_ref[...].astype(bf16)
    kf = k_ref[...] * (dqk ** -0.5)
    kb = kf.astype(bf16)
    vb = v_ref[...].astype(bf16)
    c_prev = c_ref[...]
    n_prev = n_ref[...]

    s = _dot_nt(q, kb) * p
    num = w_inter * _dot(q, c_prev.astype(bf16)) + _dot(s.astype(bf16), vb)
    qn = jnp.sum(q_ref[...] * n_prev, axis=1, keepdims=True)
    den = w_inter * qn + jnp.sum(s, axis=1, keepdims=True)
    hid = num / jnp.maximum(jnp.abs(den), jnp.exp(-m_t))

    last = lax.broadcasted_iota(jnp.int32, (L, 1), 0) == L - 1
    m_new = jnp.sum(jnp.where(last, m_t, 0.0), axis=0, keepdims=True)
    b_last = jnp.sum(jnp.where(last, b_col, 0.0), axis=0, keepdims=True)
    w_c = jnp.exp(b_last + m_prev - m_new)
    w_s = jnp.exp(b_last + u_col - m_new)
    if t_valid is not None:
        w_s = jnp.where(rvalid, w_s, 0.0)
    kw = kf * w_s
    c_ref[...] = w_c * c_prev + _dot_tn(kw.astype(bf16), vb)
    n_ref[...] = w_c * n_prev + jnp.sum(kw, axis=0, keepdims=True)
    m_ref[...] = jnp.broadcast_to(m_new, m_ref.shape)

    ms = jnp.mean(hid * hid, axis=-1, keepdims=True)
    hn = hid * lax.rsqrt(ms + RMS_EPS) * onorm_ref[...]
    y_ref[...] = (hn * _sigmoid(o_ref[...])).astype(y_ref.dtype)


def mlstm(qkvo, gates, bias, out_norm, c0, n0, m0, *, batch, chunk, t_valid=None):
    bt = qkvo.shape[0]
    t = bt // batch
    heads, dqk, dv = c0.shape[1:]
    nc = t // chunk
    assert t % chunk == 0 and dv % dqk == 0
    kq = heads
    kv = 2 * heads * dqk // dv
    ko = kv + heads
    n0r = n0.reshape(batch * heads, 1, dqk).astype(f32)
    m0r = jnp.broadcast_to(m0.reshape(batch * heads, 1, 1).astype(f32), (batch * heads, 1, LANES))
    y, c, n, m = pl.pallas_call(
        functools.partial(_mlstm_kernel, heads=heads, t_valid=t_valid),
        grid=(batch, heads, nc),
        in_specs=[
            pl.BlockSpec((chunk, dqk), lambda b, h, c: (b * nc + c, h)),
            pl.BlockSpec((chunk, dqk), lambda b, h, c: (b * nc + c, kq + h)),
            pl.BlockSpec((chunk, dv), lambda b, h, c: (b * nc + c, kv + h)),
            pl.BlockSpec((chunk, dv), lambda b, h, c: (b * nc + c, ko + h)),
            pl.BlockSpec((chunk, 2 * heads), lambda b, h, c: (b * nc + c, 0)),
            pl.BlockSpec((1, 2 * heads), lambda b, h, c: (0, 0)),
            pl.BlockSpec((None, 1, dv), lambda b, h, c: (h, 0, 0)),
            pl.BlockSpec((None, None, dqk, dv), lambda b, h, c: (b, h, 0, 0)),
            pl.BlockSpec((None, 1, dqk), lambda b, h, c: (b * heads + h, 0, 0)),
            pl.BlockSpec((None, 1, LANES), lambda b, h, c: (b * heads + h, 0, 0)),
        ],
        out_specs=[
            pl.BlockSpec((chunk, dv), lambda b, h, c: (b * nc + c, h)),
            pl.BlockSpec((None, None, dqk, dv), lambda b, h, c: (b, h, 0, 0)),
            pl.BlockSpec((None, 1, dqk), lambda b, h, c: (b * heads + h, 0, 0)),
            pl.BlockSpec((None, 1, LANES), lambda b, h, c: (b * heads + h, 0, 0)),
        ],
        out_shape=[
            jax.ShapeDtypeStruct((bt, heads * dv), bf16),
            jax.ShapeDtypeStruct((batch, heads, dqk, dv), f32),
            jax.ShapeDtypeStruct((batch * heads, 1, dqk), f32),
            jax.ShapeDtypeStruct((batch * heads, 1, LANES), f32),
        ],
        compiler_params=_cparams(3),
        name="mlstm",
    )(qkvo, qkvo, qkvo, qkvo, gates, bias.reshape(1, 2 * heads).astype(f32),
      out_norm.reshape(heads, 1, dv).astype(f32), c0.astype(f32), n0r, m0r)
    return y, c, n.reshape(batch, heads, dqk), m[:, 0, 0].reshape(batch, heads)


def _lane_cumsum(x, reverse=False):
    lane = lax.broadcasted_iota(jnp.int32, x.shape, x.ndim - 1)
    sh = 1
    while sh < LANES:
        if reverse:
            x = x + jnp.where(lane + sh < LANES, pltpu.roll(x, LANES - sh, x.ndim - 1), 0.0)
        else:
            x = x + jnp.where(lane >= sh, pltpu.roll(x, sh, x.ndim - 1), 0.0)
        sh *= 2
    return x


def _cumsum_kernel(x_ref, o_ref):
    hh, t = x_ref.shape
    carry = jnp.zeros((hh, 1), f32)
    for c in range(t // LANES):
        y = _lane_cumsum(x_ref[:, c * LANES:(c + 1) * LANES]) + carry
        o_ref[:, c * LANES:(c + 1) * LANES] = y
        carry = y[:, LANES - 1:LANES]


def cumsum_tokens(x):
    b, hh, t = x.shape
    return pl.pallas_call(
        _cumsum_kernel,
        grid=(b,),
        in_specs=[pl.BlockSpec((None, hh, t), lambda i: (i, 0, 0))],
        out_specs=pl.BlockSpec((None, hh, t), lambda i: (i, 0, 0)),
        out_shape=jax.ShapeDtypeStruct((b, hh, t), f32),
        compiler_params=_cparams(1),
        name="cumsum_tokens",
    )(x)


def _fox_prompt_kernel(q_ref, k_ref, v_ref, c_ref, o_ref, *, blk):
    qi = pl.program_id(2)
    q = q_ref[...]
    hd = q.shape[1]

    def attend(j, carry, mask):
        m, l, acc = carry
        start = pl.multiple_of(j * blk, blk)
        kb = k_ref[pl.ds(start, blk), :].astype(bf16)
        vb = v_ref[pl.ds(start, blk), :].astype(bf16)
        s = _dot_nt(q, kb) - c_ref[j]
        if mask:
            row = lax.broadcasted_iota(jnp.int32, (blk, blk), 0)
            col = lax.broadcasted_iota(jnp.int32, (blk, blk), 1)
            s = jnp.where(col <= row, s, -jnp.inf)
        m_new = jnp.maximum(m, jnp.max(s, axis=1, keepdims=True))
        alpha = jnp.exp(m - m_new)
        p = jnp.exp(s - m_new)
        l = alpha * l + jnp.sum(p, axis=1, keepdims=True)
        acc = alpha * acc + _dot(p.astype(bf16), vb)
        return m_new, l, acc

    init = (jnp.full((blk, 1), -jnp.inf, f32), jnp.zeros((blk, 1), f32), jnp.zeros((blk, hd), f32))
    carry = lax.fori_loop(0, qi, lambda j, c: attend(j, c, False), init)
    _, l, acc = attend(qi, carry, True)
    o_ref[...] = (acc / l).astype(o_ref.dtype)


def fox_prompt_attention(q, k, v, c, *, batch, heads):
    bt, width = q.shape
    t = bt // batch
    hd = width // heads
    blk = _pick(t, (FOX_Q_BLOCK, LANES))
    nq = t // blk
    c5 = c.reshape(batch, heads, nq, 1, blk)
    return pl.pallas_call(
        functools.partial(_fox_prompt_kernel, blk=blk),
        grid=(batch, heads, nq),
        in_specs=[
            pl.BlockSpec((blk, hd), lambda b, h, i: (b * nq + i, h)),
            pl.BlockSpec((t, hd), lambda b, h, i: (b, h)),
            pl.BlockSpec((t, hd), lambda b, h, i: (b, h)),
            pl.BlockSpec((None, None, nq, 1, blk), lambda b, h, i: (b, h, 0, 0, 0)),
        ],
        out_specs=pl.BlockSpec((blk, hd), lambda b, h, i: (b * nq + i, h)),
        out_shape=jax.ShapeDtypeStruct((bt, width), bf16),
        compiler_params=_cparams(3),
        name="fox_prompt_attention",
    )(q, k, v, c5)


def _fox_decay_kernel(pt_ref, lf_ref, r_ref, pad_ref, carry_ref):
    p = pl.program_id(1)
    page, hh = lf_ref.shape

    @pl.when(p == 0)
    def _():
        carry_ref[...] = jnp.zeros_like(carry_ref)
        pad_ref[...] = jnp.zeros_like(pad_ref)

    pad_ref[:, 0:hh] = lf_ref[...]
    x = pad_ref[...].T[0:hh, :]
    y = _lane_cumsum(x, reverse=True)
    carry = carry_ref[...]
    r_ref[...] = (y - x) + carry
    carry_ref[...] = carry + y[:, 0:1]


def fox_past_decay(pool_lf, layer, page_table):
    _, _, page, hh = pool_lf.shape
    b, npg = page_table.shape
    assert page == LANES
    return pl.pallas_call(
        _fox_decay_kernel,
        grid_spec=pltpu.PrefetchScalarGridSpec(
            num_scalar_prefetch=1,
            grid=(b, npg),
            in_specs=[pl.BlockSpec((None, None, page, hh),
                                   lambda i, p, pt: (layer, pt[i, npg - 1 - p], 0, 0))],
            out_specs=pl.BlockSpec((None, hh, page), lambda i, p, pt: (i, 0, npg - 1 - p)),
            scratch_shapes=[pltpu.VMEM((page, LANES), f32), pltpu.VMEM((hh, page), f32)],
        ),
        out_shape=jax.ShapeDtypeStruct((b, hh, npg * page), f32),
        compiler_params=_cparams(2),
        name="fox_past_decay",
    )(page_table, pool_lf)


def _fox_decode_kernel(pt_ref, q_ref, kp_ref, vp_ref, r_ref, kn_ref, vn_ref, a_ref, o_ref,
                       m_ref, l_ref, acc_ref, *, heads, t_valid):
    p = pl.program_id(1)
    last = pl.num_programs(1) - 1
    rows = q_ref.shape[1]
    page = r_ref.shape[1]

    @pl.when(p == 0)
    def _():
        m_ref[...] = jnp.full_like(m_ref, -jnp.inf)
        l_ref[...] = jnp.zeros_like(l_ref)
        acc_ref[...] = jnp.zeros_like(acc_ref)

    def attend(h, kb, vb, bias):
        s = _dot_nt(q_ref[h], kb) + bias
        m_old = m_ref[h]
        m_new = jnp.maximum(m_old, jnp.max(s, axis=1, keepdims=True))
        alpha = jnp.exp(m_old - m_new)
        pr = jnp.exp(s - m_new)
        l_ref[h] = alpha * l_ref[h] + jnp.sum(pr, axis=1, keepdims=True)
        acc_ref[h] = alpha * acc_ref[h] + _dot(pr.astype(bf16), vb)
        m_ref[h] = m_new

    for h in range(heads):
        kb = kp_ref[pl.ds(h, page, stride=heads), :].astype(bf16)
        vb = vp_ref[pl.ds(h, page, stride=heads), :].astype(bf16)
        attend(h, kb, vb, r_ref[h:h + 1, :])

    @pl.when(p == last)
    def _():
        row = lax.broadcasted_iota(jnp.int32, (rows, page), 0)
        col = lax.broadcasted_iota(jnp.int32, (rows, page), 1)
        allowed = (col <= row) & (col < t_valid)
        for h in range(heads):
            bias = jnp.where(allowed, -a_ref[h:h + 1, :], -jnp.inf)
            attend(h, kn_ref[h].astype(bf16), vn_ref[h].astype(bf16), bias)
            o_ref[h] = acc_ref[h] / l_ref[h]


def fox_decode_attention(q, pool_k, pool_v, layer, page_table, r, k_new, v_new, a_new, *, t_valid):
    b, heads, rows, hd = q.shape
    nl, n_pool, page, _, _ = pool_k.shape
    npg = page_table.shape[1]
    kp = pool_k.reshape(nl, n_pool, page * heads, hd)
    vp = pool_v.reshape(nl, n_pool, page * heads, hd)
    pool_spec = pl.BlockSpec((None, None, page * heads, hd), lambda i, p, pt: (layer, pt[i, p], 0, 0))
    seq4 = lambda i, p, pt: (i, 0, 0, 0)
    return pl.pallas_call(
        functools.partial(_fox_decode_kernel, heads=heads, t_valid=t_valid),
        grid_spec=pltpu.PrefetchScalarGridSpec(
            num_scalar_prefetch=1,
            grid=(b, npg),
            in_specs=[
                pl.BlockSpec((None, heads, rows, hd), seq4),
                pool_spec, pool_spec,
                pl.BlockSpec((None, heads, page), lambda i, p, pt: (i, 0, p)),
                pl.BlockSpec((None, heads, page, hd), seq4),
                pl.BlockSpec((None, heads, page, hd), seq4),
                pl.BlockSpec((None, heads, page), lambda i, p, pt: (i, 0, 0)),
            ],
            out_specs=pl.BlockSpec((None, heads, rows, hd), seq4),
            scratch_shapes=[pltpu.VMEM((heads, rows, 1), f32), pltpu.VMEM((heads, rows, 1), f32),
                            pltpu.VMEM((heads, rows, hd), f32)],
        ),
        out_shape=jax.ShapeDtypeStruct((b, heads, rows, hd), f32),
        compiler_params=_cparams(2),
        name="fox_decode_attention",
    )(page_table, q, kp, vp, r, k_new, v_new, a_new)


def _xattn_kernel(q_ref, k_ref, v_ref, o_ref, *, head_dim):
    for s in range(0, q_ref.shape[1], head_dim):
        q = q_ref[:, s:s + head_dim]
        kb = k_ref[:, s:s + head_dim].astype(bf16)
        vb = v_ref[:, s:s + head_dim].astype(bf16)
        sc = _dot_nt(q, kb)
        m = jnp.max(sc, axis=1, keepdims=True)
        p = jnp.exp(sc - m)
        l = jnp.sum(p, axis=1, keepdims=True)
        o_ref[:, s:s + head_dim] = (_dot(p.astype(bf16), vb) / l).astype(o_ref.dtype)


def mem_cross_attention(q, mk, mv, *, head_dim):
    b, t, w = q.shape
    mt = mk.shape[1]
    bq = _pick(t, (512, 256, 128))
    return pl.pallas_call(
        functools.partial(_xattn_kernel, head_dim=head_dim),
        grid=(b, t // bq),
        in_specs=[pl.BlockSpec((None, bq, w), lambda i, j: (i, j, 0)),
                  pl.BlockSpec((None, mt, w), lambda i, j: (i, 0, 0)),
                  pl.BlockSpec((None, mt, w), lambda i, j: (i, 0, 0))],
        out_specs=pl.BlockSpec((None, bq, w), lambda i, j: (i, j, 0)),
        out_shape=jax.ShapeDtypeStruct((b, t, w), bf16),
        compiler_params=_cparams(2),
        name="mem_cross_attention",
    )(q, mk, mv)


def _ffn(y, norm, w_in, w_out):
    h = matmul_swiglu(rmsnorm(y, norm), w_in)
    bm = 512 if y.shape[0] % 512 == 0 else None
    return matmul_residual(h, w_out, y, FFN_RES_SCALE, bm=bm)


def _mlstm_mixer(y, norm, w_main, w_gate, b_gate, out_norm, w_out, c0, n0, m0, *, batch, chunk, t_valid):
    hn = rmsnorm(y, norm)
    qkvo = matmul(hn, w_main)
    gates = matmul(hn, w_gate, bn=w_gate.shape[1])
    g, c, n, m = mlstm(qkvo, gates, b_gate, out_norm, c0, n0, m0, batch=batch, chunk=chunk, t_valid=t_valid)
    return matmul_residual(g, w_out, y, 1.0), c, n, m


def _fox_project(y, norm, w_qkv, w_f, b_f, q_norm, k_norm):
    hn = rmsnorm(y, norm)
    width = w_qkv.shape[1] // 3
    hd = q_norm.shape[-1]
    q = matmul_headnorm(hn, w_qkv, q_norm, col0=0, n=width, scale=hd ** -0.5, out_dtype=bf16)
    k = matmul_headnorm(hn, w_qkv, k_norm, col0=width, n=width)
    v = matmul(hn, w_qkv, col0=2 * width, n=width)
    lf = matmul_logsigmoid(hn, w_f, b_f)
    return q, k, v, lf


def _pad_lanes(x):
    t = x.shape[-1]
    tp = -(-t // LANES) * LANES
    return x if tp == t else jnp.pad(x, [(0, 0)] * (x.ndim - 1) + [(0, tp - t)])


def _xattn(y, norm, w_q, q_norm, w_o, mk, mv, *, batch):
    hd = q_norm.shape[-1]
    w = w_q.shape[1]
    q = matmul_headnorm(rmsnorm(y, norm), w_q, q_norm, scale=hd ** -0.5, out_dtype=bf16)
    o = mem_cross_attention(q.reshape(batch, -1, w), mk, mv, head_dim=hd)
    return matmul_residual(o.reshape(-1, w), w_o, y, 1.0)


def kernel(x_prompt, x_sample, state_mlstm_C, state_mlstm_n, state_mlstm_m, cache_fox_k, cache_fox_v, cache_fox_logf, cache_mem_k, cache_mem_v, page_table, mem_prompt, norm_ffn1, w_ffn1_in, w_ffn1_out, norm_mix, w_mlstm_in, b_mlstm_gate, mlstm_out_norm, w_mlstm_out, w_fox_in, b_fox_f, fox_q_norm, fox_k_norm, w_fox_out, norm_xattn, norm_mem, w_xq, w_xkv, xq_norm, xk_norm, w_xo, norm_ffn2, w_ffn2_in, w_ffn2_out):
    bp, seq, d = x_prompt.shape
    bs, dec_seq, _ = x_sample.shape
    depth = norm_ffn1.shape[0]
    n_mixers = 2
    m_heads = b_mlstm_gate.shape[-1] // 2
    m_dv = mlstm_out_norm.shape[-1]
    m_dqk = state_mlstm_n.shape[-1]
    m_main = 2 * m_heads * (m_dqk + m_dv)
    f_heads = b_fox_f.shape[-1]
    f_hd = fox_q_norm.shape[-1]
    f_width = f_heads * f_hd
    page = cache_fox_k.shape[2]
    x_heads, x_hd = cache_mem_k.shape[3], cache_mem_k.shape[4]
    x_width = x_heads * x_hd
    mem_tokens = mem_prompt.shape[1]
    rows = SAMPLE_ROWS
    assert dec_seq <= rows and page == LANES

    yp = x_prompt.reshape(bp * seq, d)
    ys = jnp.pad(x_sample, ((0, 0), (0, rows - dec_seq), (0, 0))).reshape(bs * rows, d)
    memp = mem_prompt.reshape(bp * mem_tokens, d)
    chunk_p = _pick(seq, (MLSTM_PROMPT_CHUNK, 128, 64))

    c_p, n_p, m_p, c_s, n_s, m_s = [], [], [], [], [], []
    fk_p, fv_p, fl_p, fk_s, fv_s, fl_s = [], [], [], [], [], []
    mk_p, mv_p = [], []
    for layer in range(depth):
        j = layer // n_mixers
        w1i, w1o = w_ffn1_in[layer].astype(bf16), w_ffn1_out[layer].astype(bf16)
        yp = _ffn(yp, norm_ffn1[layer], w1i, w1o)
        ys = _ffn(ys, norm_ffn1[layer], w1i, w1o)

        if layer % n_mixers == 0:
            w_main = w_mlstm_in[j][:, :m_main].astype(bf16)
            w_gate = w_mlstm_in[j][:, m_main:].astype(bf16)
            w_out = w_mlstm_out[j].astype(bf16)
            zc = jnp.zeros((bp, m_heads, m_dqk, m_dv), f32)
            zn = jnp.zeros((bp, m_heads, m_dqk), f32)
            zm = jnp.zeros((bp, m_heads), f32)
            yp, c1, n1, m1 = _mlstm_mixer(yp, norm_mix[layer], w_main, w_gate, b_mlstm_gate[j], mlstm_out_norm[j],
                                          w_out, zc, zn, zm, batch=bp, chunk=chunk_p, t_valid=None)
            ys, c2, n2, m2 = _mlstm_mixer(ys, norm_mix[layer], w_main, w_gate, b_mlstm_gate[j], mlstm_out_norm[j],
                                          w_out, state_mlstm_C[j], state_mlstm_n[j], state_mlstm_m[j],
                                          batch=bs, chunk=rows, t_valid=dec_seq)
            c_p.append(c1); n_p.append(n1); m_p.append(m1)
            c_s.append(c2); n_s.append(n2); m_s.append(m2)
        else:
            w_qkv = w_fox_in[j][:, :3 * f_width].astype(bf16)
            w_f = w_fox_in[j][:, 3 * f_width:].astype(bf16)
            w_out = w_fox_out[j].astype(bf16)
            q, k, v, lf = _fox_project(yp, norm_mix[layer], w_qkv, w_f, b_fox_f[j], fox_q_norm[j], fox_k_norm[j])
            c = cumsum_tokens(lf.reshape(bp, seq, f_heads).transpose(0, 2, 1))
            o = fox_prompt_attention(q, k, v, c, batch=bp, heads=f_heads)
            yp = matmul_residual(o, w_out, yp, 1.0)
            fk_p.append(k.reshape(bp, seq // page, page, f_heads, f_hd))
            fv_p.append(v.reshape(bp, seq // page, page, f_heads, f_hd))
            fl_p.append(lf.reshape(bp, seq // page, page, f_heads))
            q, k, v, lf = _fox_project(ys, norm_mix[layer], w_qkv, w_f, b_fox_f[j], fox_q_norm[j], fox_k_norm[j])
            a = cumsum_tokens(_pad_lanes(lf.reshape(bs, rows, f_heads).transpose(0, 2, 1)))
            r = fox_past_decay(cache_fox_logf, j, page_table)

            def heads_first(t, pad_to=None):
                t = t.reshape(bs, rows, f_heads, f_hd).transpose(0, 2, 1, 3)
                if pad_to:
                    t = jnp.pad(t, ((0, 0), (0, 0), (0, pad_to - rows), (0, 0)))
                return t

            o = fox_decode_attention(heads_first(q), cache_fox_k, cache_fox_v, j, page_table, r,
                                     heads_first(k, page), heads_first(v, page), a, t_valid=dec_seq)
            o = o.transpose(0, 2, 1, 3).reshape(bs * rows, f_width).astype(bf16)
            ys = matmul_residual(o, w_out, ys, 1.0)
            fk_s.append(k.reshape(bs, rows, f_heads, f_hd)[:, :dec_seq])
            fv_s.append(v.reshape(bs, rows, f_heads, f_hd)[:, :dec_seq])
            fl_s.append(lf.reshape(bs, rows, f_heads)[:, :dec_seq])

        w_kv = w_xkv[layer].astype(bf16)
        mem_n = rmsnorm(memp, norm_mem[layer])
        mk = matmul_headnorm(mem_n, w_kv, xk_norm[layer], col0=0, n=x_width)
        mv = matmul(mem_n, w_kv, col0=x_width, n=x_width)
        mk_p.append(mk.reshape(bp, mem_tokens, x_heads, x_hd))
        mv_p.append(mv.reshape(bp, mem_tokens, x_heads, x_hd))
        wq, wo = w_xq[layer].astype(bf16), w_xo[layer].astype(bf16)
        yp = _xattn(yp, norm_xattn[layer], wq, xq_norm[layer], wo,
                    mk.reshape(bp, mem_tokens, x_width), mv.reshape(bp, mem_tokens, x_width), batch=bp)
        ys = _xattn(ys, norm_xattn[layer], wq, xq_norm[layer], wo,
                    cache_mem_k[layer].reshape(bs, mem_tokens, x_width),
                    cache_mem_v[layer].reshape(bs, mem_tokens, x_width), batch=bs)

        w2i, w2o = w_ffn2_in[layer].astype(bf16), w_ffn2_out[layer].astype(bf16)
        yp = _ffn(yp, norm_ffn2[layer], w2i, w2o)
        ys = _ffn(ys, norm_ffn2[layer], w2i, w2o)

    yp = yp.reshape(bp, seq, d)
    ys = ys.reshape(bs, rows, d)[:, :dec_seq]
    return (yp, ys, jnp.stack(c_p), jnp.stack(n_p), jnp.stack(m_p), jnp.stack(fk_p), jnp.stack(fv_p),
            jnp.stack(fl_p), jnp.stack(mk_p), jnp.stack(mv_p), jnp.stack(c_s), jnp.stack(n_s), jnp.stack(m_s),
            jnp.stack(fk_s), jnp.stack(fv_s), jnp.stack(fl_s))
```

```python
import functools
import math

import jax
import jax.numpy as jnp
from jax import lax
from jax.experimental import pallas as pl
from jax.experimental.pallas import tpu as pltpu

RMS_EPS = 1e-6
FFN_RES_SCALE = 0.5
LOG2E = math.log2(math.e)
LANES = 128
SUBLANES = 8
SAMPLE_ROWS = 16
VMEM_LIMIT_BYTES = 56 * 1024 * 1024
MLSTM_PROMPT_CHUNK = 256
FOX_Q_BLOCK = 256
FOX_HEADS_PER_STEP = 2
DECODE_PAGES_PER_STEP = 2
DECAY_PAGES_PER_STEP = 16

f32 = jnp.float32
bf16 = jnp.bfloat16


def _cparams(n_axes):
    return pltpu.CompilerParams(
        dimension_semantics=("arbitrary",) * n_axes,
        vmem_limit_bytes=VMEM_LIMIT_BYTES)


def _dot(a, b):
    return jnp.dot(a, b, preferred_element_type=f32)


def _dot_nt(a, b):
    return lax.dot_general(a, b, (((1,), (1,)), ((), ())), preferred_element_type=f32)


def _dot_tn(a, b):
    return lax.dot_general(a, b, (((0,), (0,)), ((), ())), preferred_element_type=f32)


def _sigmoid(x):
    return 1.0 / (1.0 + jnp.exp(-x))


def _log_sigmoid(x):
    return -(jnp.maximum(-x, 0.0) + jnp.log(1.0 + jnp.exp(-jnp.abs(x))))


def _pick(n, candidates):
    for c in candidates:
        if n % c == 0:
            return c
    return n


def _rmsnorm_kernel(x_ref, g_ref, o_ref):
    x = x_ref[...]
    ms = jnp.mean(x * x, axis=-1, keepdims=True)
    o_ref[...] = (x * lax.rsqrt(ms + RMS_EPS) * g_ref[...]).astype(o_ref.dtype)


def rmsnorm(x, g):
    m, d = x.shape
    bm = _pick(m, (256, 128))
    return pl.pallas_call(
        _rmsnorm_kernel,
        grid=(m // bm,),
        in_specs=[pl.BlockSpec((bm, d), lambda i: (i, 0)),
                  pl.BlockSpec((1, d), lambda i: (0, 0))],
        out_specs=pl.BlockSpec((bm, d), lambda i: (i, 0)),
        out_shape=jax.ShapeDtypeStruct((m, d), bf16),
        compiler_params=_cparams(1),
        name="rmsnorm",
    )(x, g.reshape(1, d).astype(f32))


def _epi_plain(accs, shared, extras, outs):
    for o in outs:
        o[...] = accs[0].astype(o.dtype)


def _epi_residual(accs, shared, extras, outs, *, scale):
    acc = accs[0] if scale == 1.0 else scale * accs[0]
    outs[0][...] = extras[0][...] + acc


def _epi_swiglu(accs, shared, extras, outs):
    g, u = accs
    outs[0][...] = (g * _sigmoid(g) * u).astype(outs[0].dtype)


def _epi_headnorm(accs, shared, extras, outs, *, scale, head_dim):
    acc = accs[0]
    g = shared[0][...]
    for s in range(0, acc.shape[1], head_dim):
        a = acc[:, s:s + head_dim]
        ms = jnp.mean(a * a, axis=-1, keepdims=True)
        y = a * lax.rsqrt(ms + RMS_EPS) * g
        if scale != 1.0:
            y = y * scale
        for o in outs:
            o[:, s:s + head_dim] = y.astype(o.dtype)


def _proj_kernel(*refs, n_w, n_shared, n_extra, n_out, has_s, has_side, epilogue):
    it = iter(refs)
    xp = next(it)
    xs = next(it) if has_s else None
    ws = [next(it) for _ in range(n_w)]
    shared = [next(it) for _ in range(n_shared)]
    ext_p = [next(it) for _ in range(n_extra)]
    ext_s = [next(it) for _ in range(n_extra)] if has_s else None
    side_in = next(it) if has_side else None
    out_p = [next(it) for _ in range(n_out)]
    out_s = [next(it) for _ in range(n_out)] if has_s else None
    side_out = next(it) if has_side else None
    wscr = [next(it) for _ in range(n_w)]

    @pl.when(pl.program_id(1) == 0)
    def _():
        for w, s in zip(ws, wscr):
            s[...] = w[...].astype(bf16)
        if has_s:
            x = xs[...]
            epilogue([_dot(x, s[...]) for s in wscr], shared, ext_s, out_s)

    x = xp[...]
    epilogue([_dot(x, s[...]) for s in wscr], shared, ext_p, out_p)
    if has_side:
        side_out[...] = side_in[...].astype(bf16)


def _project(xp, xs, w, *, col0s, n, bn, epilogue, out_dtypes, shared=(), extras_p=(), extras_s=(),
             side=None, bm=None, name):
    mp, k = xp.shape
    bm = bm or _pick(mp, (1024, 512, 256, 128))
    assert mp % bm == 0 and n % bn == 0 and all(c % bn == 0 for c in col0s)
    nj, ni = n // bn, mp // bm
    has_s = xs is not None
    ms = xs.shape[0] if has_s else 0
    n_out = len(out_dtypes)

    in_specs = [pl.BlockSpec((bm, k), lambda j, i: (i, 0))]
    args = [xp]
    if has_s:
        in_specs.append(pl.BlockSpec((ms, k), lambda j, i: (0, 0)))
        args.append(xs)
    for c in col0s:
        in_specs.append(pl.BlockSpec((k, bn), lambda j, i, c0=c // bn: (0, j + c0)))
        args.append(w)
    for s in shared:
        in_specs.append(pl.BlockSpec(s.shape, lambda j, i: (0, 0)))
        args.append(s)
    for e in extras_p:
        in_specs.append(pl.BlockSpec((bm, bn), lambda j, i: (i, j)))
        args.append(e)
    if has_s:
        for e in extras_s:
            in_specs.append(pl.BlockSpec((ms, bn), lambda j, i: (0, j)))
            args.append(e)
    out_specs = [pl.BlockSpec((bm, bn), lambda j, i: (i, j)) for _ in out_dtypes]
    out_shape = [jax.ShapeDtypeStruct((mp, n), d) for d in out_dtypes]
    if has_s:
        out_specs += [pl.BlockSpec((ms, bn), lambda j, i: (0, j)) for _ in out_dtypes]
        out_shape += [jax.ShapeDtypeStruct((ms, n), d) for d in out_dtypes]
    has_side = side is not None
    if has_side:
        rows, cols = side.shape
        slab = rows // (nj * ni)
        assert slab * nj * ni == rows and slab % 16 == 0
        in_specs.append(pl.BlockSpec((slab, cols), lambda j, i: (j * ni + i, 0)))
        args.append(side)
        out_specs.append(pl.BlockSpec((slab, cols), lambda j, i: (j * ni + i, 0)))
        out_shape.append(jax.ShapeDtypeStruct((rows, cols), bf16))

    outs = pl.pallas_call(
        functools.partial(_proj_kernel, n_w=len(col0s), n_shared=len(shared), n_extra=len(extras_p),
                          n_out=n_out, has_s=has_s, has_side=has_side, epilogue=epilogue),
        grid=(nj, ni),
        in_specs=in_specs,
        out_specs=out_specs,
        out_shape=out_shape,
        scratch_shapes=[pltpu.VMEM((k, bn), bf16) for _ in col0s],
        compiler_params=_cparams(2),
        name=name,
    )(*args)
    return list(outs)


def _side_ok(rows, n, bn, mp, bm):
    steps = (n // bn) * (mp // bm)
    return rows % steps == 0 and (rows // steps) % 16 == 0


def _mm_residual_kernel(x_ref, w_ref, r_ref, o_ref, *, scale):
    acc = _dot(x_ref[...], w_ref[...])
    if scale != 1.0:
        acc = scale * acc
    o_ref[...] = r_ref[...] + acc


def matmul_residual(x, w, res, scale, *, bm=None, bn=None):
    m, k = x.shape
    n = w.shape[1]
    bm = bm or _pick(m, (512, 256, 128))
    bn = bn or _pick(n, (512, 256, 128))
    return pl.pallas_call(
        functools.partial(_mm_residual_kernel, scale=scale),
        grid=(m // bm, n // bn),
        in_specs=[pl.BlockSpec((bm, k), lambda i, j: (i, 0)),
                  pl.BlockSpec((k, bn), lambda i, j: (0, j)),
                  pl.BlockSpec((bm, bn), lambda i, j: (i, j))],
        out_specs=pl.BlockSpec((bm, bn), lambda i, j: (i, j)),
        out_shape=jax.ShapeDtypeStruct((m, n), f32),
        compiler_params=_cparams(2),
        name="matmul_residual",
    )(x, w, res)


def _mm_narrow_kernel(x_ref, w_ref, b_ref, o_ref, *, log_sigmoid):
    acc = _dot(x_ref[...], w_ref[...].astype(bf16))
    o_ref[...] = _log_sigmoid(acc + b_ref[...]) if log_sigmoid else acc


def matmul_narrow(x, w, b=None, *, log_sigmoid=False):
    m, k = x.shape
    n = w.shape[1]
    bm = _pick(m, (1024, 512, 256, 128))
    b = jnp.zeros((n,), f32) if b is None else b
    return pl.pallas_call(
        functools.partial(_mm_narrow_kernel, log_sigmoid=log_sigmoid),
        grid=(m // bm,),
        in_specs=[pl.BlockSpec((bm, k), lambda i: (i, 0)),
                  pl.BlockSpec((k, n), lambda i: (0, 0)),
                  pl.BlockSpec((1, n), lambda i: (0, 0))],
        out_specs=pl.BlockSpec((bm, n), lambda i: (i, 0)),
        out_shape=jax.ShapeDtypeStruct((m, n), f32),
        compiler_params=_cparams(1),
        name="matmul_narrow",
    )(x, w, b.reshape(1, n).astype(f32))


def _mlstm_kernel(q_ref, k_ref, v_ref, o_ref, gates_ref, bias_ref, onorm_ref,
                  c0_ref, n0_ref, m0_ref,
                  y_ref, c_ref, n_ref, m_ref, *, heads, t_valid):
    h = pl.program_id(1)
    ci = pl.program_id(2)
    L, dqk = q_ref.shape

    @pl.when(ci == 0)
    def _():
        c_ref[...] = c0_ref[...]
        n_ref[...] = n0_ref[...]
        m_ref[...] = m0_ref[...]

    row = lax.broadcasted_iota(jnp.int32, (L, L), 0)
    col = lax.broadcasted_iota(jnp.int32, (L, L), 1)
    eye = row == col
    causal = col <= row

    gates = gates_ref[...] + bias_ref[...]
    lane = lax.broadcasted_iota(jnp.int32, gates.shape, 1)
    ig_col = jnp.sum(jnp.where(lane == h, gates, 0.0), axis=1, keepdims=True)
    f_col = jnp.sum(jnp.where(lane == h + heads, gates, 0.0), axis=1, keepdims=True)
    lf_col = _log_sigmoid(f_col)
    if t_valid is not None:
        rvalid = lax.broadcasted_iota(jnp.int32, (L, 1), 0) < t_valid
        lf_col = jnp.where(rvalid, lf_col, 0.0)
        ig_col = jnp.where(rvalid, ig_col, -jnp.inf)

    b_row = jnp.sum(jnp.where(row <= col, lf_col, 0.0), axis=0, keepdims=True)
    lf_row = jnp.sum(jnp.where(eye, lf_col, 0.0), axis=0, keepdims=True)
    b_col = jnp.sum(jnp.where(causal, lf_row, 0.0), axis=1, keepdims=True)
    ig_row = jnp.sum(jnp.where(eye, ig_col, 0.0), axis=0, keepdims=True)
    u_row = ig_row - b_row
    u_col = ig_col - b_col

    m_prev = m_ref[:, 0:1]
    d = jnp.where(causal, b_col + u_row, -jnp.inf)
    inter = b_col + m_prev
    m_t = jnp.maximum(jnp.max(d, axis=1, keepdims=True), inter)
    p = jnp.exp(d - m_t)
    w_inter = jnp.exp(inter - m_t)

    q = q_ref[...].astype(bf16)
    kf = k_ref[...] * (dqk ** -0.5)
    kb = kf.astype(bf16)
    vb = v_ref[...].astype(bf16)
    c_prev = c_ref[...]
    n_prev = n_ref[...]

    s = _dot_nt(q, kb) * p
    num = w_inter * _dot(q, c_prev.astype(bf16)) + _dot(s.astype(bf16), vb)
    qn = jnp.sum(q_ref[...] * n_prev, axis=1, keepdims=True)
    den = w_inter * qn + jnp.sum(s, axis=1, keepdims=True)
    hid = num / jnp.maximum(jnp.abs(den), jnp.exp(-m_t))

    last = lax.broadcasted_iota(jnp.int32, (L, 1), 0) == L - 1
    m_new = jnp.sum(jnp.where(last, m_t, 0.0), axis=0, keepdims=True)
    b_last = jnp.sum(jnp.where(last, b_col, 0.0), axis=0, keepdims=True)
    w_c = jnp.exp(b_last + m_prev - m_new)
    w_s = jnp.exp(b_last + u_col - m_new)
    if t_valid is not None:
        w_s = jnp.where(rvalid, w_s, 0.0)
    kw = kf * w_s
    c_ref[...] = w_c * c_prev + _dot_tn(kw.astype(bf16), vb)
    n_ref[...] = w_c * n_prev + jnp.sum(kw, axis=0, keepdims=True)
    m_ref[...] = jnp.broadcast_to(m_new, m_ref.shape)

    ms = jnp.mean(hid * hid, axis=-1, keepdims=True)
    hn = hid * lax.rsqrt(ms + RMS_EPS) * onorm_ref[...]
    y_ref[...] = (hn * _sigmoid(o_ref[...])).astype(y_ref.dtype)


def mlstm(qkvo, gates, bias, out_norm, c0, n0, m0, *, batch, chunk, t_valid=None):
    bt = qkvo.shape[0]
    t = bt // batch
    heads, dqk, dv = c0.shape[1:]
    nc = t // chunk
    assert t % chunk == 0 and dv % dqk == 0
    kq = heads
    kv = 2 * heads * dqk // dv
    ko = kv + heads
    n0r = n0.reshape(batch * heads, 1, dqk).astype(f32)
    m0r = jnp.broadcast_to(m0.reshape(batch * heads, 1, 1).astype(f32), (batch * heads, 1, LANES))
    y, c, n, m = pl.pallas_call(
        functools.partial(_mlstm_kernel, heads=heads, t_valid=t_valid),
        grid=(batch, heads, nc),
        in_specs=[
            pl.BlockSpec((chunk, dqk), lambda b, h, c: (b * nc + c, h)),
            pl.BlockSpec((chunk, dqk), lambda b, h, c: (b * nc + c, kq + h)),
            pl.BlockSpec((chunk, dv), lambda b, h, c: (b * nc + c, kv + h)),
            pl.BlockSpec((chunk, dv), lambda b, h, c: (b * nc + c, ko + h)),
            pl.BlockSpec((chunk, 2 * heads), lambda b, h, c: (b * nc + c, 0)),
            pl.BlockSpec((1, 2 * heads), lambda b, h, c: (0, 0)),
            pl.BlockSpec((None, 1, dv), lambda b, h, c: (h, 0, 0)),
            pl.BlockSpec((None, None, dqk, dv), lambda b, h, c: (b, h, 0, 0)),
            pl.BlockSpec((None, 1, dqk), lambda b, h, c: (b * heads + h, 0, 0)),
            pl.BlockSpec((None, 1, LANES), lambda b, h, c: (b * heads + h, 0, 0)),
        ],
        out_specs=[
            pl.BlockSpec((chunk, dv), lambda b, h, c: (b * nc + c, h)),
            pl.BlockSpec((None, None, dqk, dv), lambda b, h, c: (b, h, 0, 0)),
            pl.BlockSpec((None, 1, dqk), lambda b, h, c: (b * heads + h, 0, 0)),
            pl.BlockSpec((None, 1, LANES), lambda b, h, c: (b * heads + h, 0, 0)),
        ],
        out_shape=[
            jax.ShapeDtypeStruct((bt, heads * dv), bf16),
            jax.ShapeDtypeStruct((batch, heads, dqk, dv), f32),
            jax.ShapeDtypeStruct((batch * heads, 1, dqk), f32),
            jax.ShapeDtypeStruct((batch * heads, 1, LANES), f32),
        ],
        compiler_params=_cparams(3),
        name="mlstm",
    )(qkvo, qkvo, qkvo, qkvo, gates, bias.reshape(1, 2 * heads).astype(f32),
      out_norm.reshape(heads, 1, dv).astype(f32), c0.astype(f32), n0r, m0r)
    return y, c, n.reshape(batch, heads, dqk), m[:, 0, 0].reshape(batch, heads)


def _lane_cumsum(x, reverse=False):
    lane = lax.broadcasted_iota(jnp.int32, x.shape, x.ndim - 1)
    sh = 1
    while sh < LANES:
        if reverse:
            x = x + jnp.where(lane + sh < LANES, pltpu.roll(x, LANES - sh, x.ndim - 1), 0.0)
        else:
            x = x + jnp.where(lane >= sh, pltpu.roll(x, sh, x.ndim - 1), 0.0)
        sh *= 2
    return x


def _cumsum_kernel(x_ref, o_ref):
    hh, t = x_ref.shape
    carry = jnp.zeros((hh, 1), f32)
    for c in range(t // LANES):
        y = _lane_cumsum(x_ref[:, c * LANES:(c + 1) * LANES]) + carry
        o_ref[:, c * LANES:(c + 1) * LANES] = y * LOG2E
        carry = y[:, LANES - 1:LANES]


def cumsum_tokens_log2(x):
    b, hh, t = x.shape
    return pl.pallas_call(
        _cumsum_kernel,
        grid=(b,),
        in_specs=[pl.BlockSpec((None, hh, t), lambda i: (i, 0, 0))],
        out_specs=pl.BlockSpec((None, hh, t), lambda i: (i, 0, 0)),
        out_shape=jax.ShapeDtypeStruct((b, hh, t), f32),
        compiler_params=_cparams(1),
        name="cumsum_tokens",
    )(x)


def _fox_prompt_kernel(q_ref, k_ref, v_ref, c_ref, o_ref, *, blk, hd, n_heads):
    qi = pl.program_id(2)

    def attend(blocks, carry):
        out = []
        for hh in range(n_heads):
            m, l, acc = carry[hh]
            cols = slice(hh * hd, (hh + 1) * hd)
            q = q_ref[:, cols]
            scores = []
            for j, mask in blocks:
                start = pl.multiple_of(j * blk, blk)
                s = _dot_nt(q, k_ref[pl.ds(start, blk), cols]) - c_ref[hh, j]
                if mask:
                    row = lax.broadcasted_iota(jnp.int32, (blk, blk), 0)
                    col = lax.broadcasted_iota(jnp.int32, (blk, blk), 1)
                    s = jnp.where(col <= row, s, -jnp.inf)
                scores.append(s)
            m_new = m
            for s in scores:
                m_new = jnp.maximum(m_new, jnp.max(s, axis=1, keepdims=True))
            alpha = jnp.exp2(m - m_new)
            l = alpha * l
            acc = alpha * acc
            for (j, _), s in zip(blocks, scores):
                start = pl.multiple_of(j * blk, blk)
                p = jnp.exp2(s - m_new)
                l = l + jnp.sum(p, axis=1, keepdims=True)
                acc = acc + _dot(p.astype(bf16), v_ref[pl.ds(start, blk), cols])
            out.append((m_new, l, acc))
        return tuple(out)

    init = tuple((jnp.full((blk, 1), -jnp.inf, f32), jnp.zeros((blk, 1), f32), jnp.zeros((blk, hd), f32))
                 for _ in range(n_heads))
    carry = lax.fori_loop(0, qi // 2, lambda j, c: attend([(2 * j, False), (2 * j + 1, False)], c), init)
    carry = lax.cond(qi % 2 == 1,
                     lambda c: attend([(qi - 1, False), (qi, True)], c),
                     lambda c: attend([(qi, True)], c), carry)
    for hh in range(n_heads):
        _, l, acc = carry[hh]
        o_ref[:, hh * hd:(hh + 1) * hd] = (acc / l).astype(o_ref.dtype)


def fox_prompt_attention(q, k, v, c, *, batch, heads):
    bt, width = q.shape
    t = bt // batch
    hd = width // heads
    blk = _pick(t, (FOX_Q_BLOCK, LANES))
    nq = t // blk
    nh = FOX_HEADS_PER_STEP if heads % FOX_HEADS_PER_STEP == 0 else 1
    c5 = c.reshape(batch, heads, nq, 1, blk)
    return pl.pallas_call(
        functools.partial(_fox_prompt_kernel, blk=blk, hd=hd, n_heads=nh),
        grid=(batch, heads // nh, nq),
        in_specs=[
            pl.BlockSpec((blk, nh * hd), lambda b, h, i: (b * nq + i, h)),
            pl.BlockSpec((t, nh * hd), lambda b, h, i: (b, h)),
            pl.BlockSpec((t, nh * hd), lambda b, h, i: (b, h)),
            pl.BlockSpec((None, nh, nq, 1, blk), lambda b, h, i: (b, h, 0, 0, 0)),
        ],
        out_specs=pl.BlockSpec((blk, nh * hd), lambda b, h, i: (b * nq + i, h)),
        out_shape=jax.ShapeDtypeStruct((bt, width), bf16),
        compiler_params=_cparams(3),
        name="fox_prompt_attention",
    )(q, k, v, c5)


def _fox_decay_kernel(pt_ref, *refs, n_pages):
    lf_refs = refs[:n_pages]
    r_ref, pad_ref, carry_ref = refs[n_pages:]
    page, hh = lf_refs[0].shape

    @pl.when(pl.program_id(1) == 0)
    def _():
        carry_ref[...] = jnp.zeros_like(carry_ref)
        pad_ref[...] = jnp.zeros_like(pad_ref)

    carry = carry_ref[...]
    for g in reversed(range(n_pages)):
        pad_ref[:, 0:hh] = lf_refs[g][...]
        x = pad_ref[...].T[0:hh, :]
        y = _lane_cumsum(x, reverse=True)
        r_ref[:, g * page:(g + 1) * page] = ((y - x) + carry) * LOG2E
        carry = carry + y[:, 0:1]
    carry_ref[...] = carry


def fox_past_decay_log2(pool_lf, layer, page_table):
    _, _, page, hh = pool_lf.shape
    b, npg = page_table.shape
    g = _pick(npg, (DECAY_PAGES_PER_STEP, 8, 4, 2, 1))
    ns = npg // g
    assert page == LANES

    def lf_spec(gi):
        return pl.BlockSpec((None, None, page, hh),
                            lambda i, s, pt: (layer, pt[i, (ns - 1 - s) * g + gi], 0, 0))

    return pl.pallas_call(
        functools.partial(_fox_decay_kernel, n_pages=g),
        grid_spec=pltpu.PrefetchScalarGridSpec(
            num_scalar_prefetch=1,
            grid=(b, ns),
            in_specs=[lf_spec(gi) for gi in range(g)],
            out_specs=pl.BlockSpec((None, hh, g * page), lambda i, s, pt: (i, 0, ns - 1 - s)),
            scratch_shapes=[pltpu.VMEM((page, LANES), f32), pltpu.VMEM((hh, page), f32)],
        ),
        out_shape=jax.ShapeDtypeStruct((b, hh, npg * page), f32),
        compiler_params=_cparams(2),
        name="fox_past_decay",
    )(page_table, *([pool_lf] * g))


def _fox_decode_kernel(pt_ref, q_ref, *refs, n_pages, t_valid, ht):
    kps = refs[:n_pages]
    vps = refs[n_pages:2 * n_pages]
    r_ref, kn_ref, vn_ref, a_ref, o_ref, m_ref, l_ref, acc_ref = refs[2 * n_pages:]
    p = pl.program_id(1)
    last = pl.num_programs(1) - 1
    tq, heads, hd = q_ref.shape
    nt = heads // ht
    rq = tq * ht
    page = kps[0].shape[0]
    rows_n = kn_ref.shape[0]

    @pl.when(p == 0)
    def _():
        m_ref[...] = jnp.full_like(m_ref, -jnp.inf)
        l_ref[...] = jnp.zeros_like(l_ref)
        acc_ref[...] = jnp.zeros_like(acc_ref)

    def tile(ref, a):
        x = ref[:, a * ht:(a + 1) * ht, :]
        return x.reshape(x.shape[0] * ht, hd).astype(bf16)

    def attend(a, qa, blocks):
        scores = [_dot_nt(qa, kb) + bias for kb, _, bias in blocks]
        m_old = m_ref[a]
        m_new = m_old
        for s in scores:
            m_new = jnp.maximum(m_new, jnp.max(s, axis=1, keepdims=True))
        alpha = jnp.exp2(m_old - m_new)
        l = alpha * l_ref[a]
        acc = alpha * acc_ref[a]
        for (_, vb, _), s in zip(blocks, scores):
            pr = jnp.exp2(s - m_new)
            l = l + jnp.sum(pr, axis=1, keepdims=True)
            acc = acc + _dot(pr.astype(bf16), vb)
        l_ref[a] = l
        acc_ref[a] = acc
        m_ref[a] = m_new

    row = lax.broadcasted_iota(jnp.int32, (rq, page * ht), 0)
    col = lax.broadcasted_iota(jnp.int32, (rq, page * ht), 1)
    head_mask = jnp.where((col & (ht - 1)) == (row & (ht - 1)), 0.0, -jnp.inf)
    for a in range(nt):
        attend(a, tile(q_ref, a),
               [(tile(kps[g], a), tile(vps[g], a), r_ref[g, a] + head_mask) for g in range(n_pages)])

    @pl.when(p == last)
    def _():
        shift = ht.bit_length() - 1
        rown = lax.broadcasted_iota(jnp.int32, (rq, rows_n * ht), 0)
        coln = lax.broadcasted_iota(jnp.int32, (rq, rows_n * ht), 1)
        tk = coln >> shift
        allowed = ((coln & (ht - 1)) == (rown & (ht - 1))) & (tk <= (rown >> shift)) & (tk < t_valid)
        for a in range(nt):
            bias = jnp.where(allowed, -a_ref[a], -jnp.inf)
            attend(a, tile(q_ref, a), [(tile(kn_ref, a), tile(vn_ref, a), bias)])
            o_ref[a] = acc_ref[a] / l_ref[a]


def fox_decode_attention(q, pool_k, pool_v, layer, page_table, r, k_new, v_new, a_new, *, t_valid):
    b, tq, heads, hd = q.shape
    page = pool_k.shape[2]
    npg = page_table.shape[1]
    rows_n = k_new.shape[1]
    ht = SUBLANES
    assert heads % ht == 0
    nt = heads // ht
    g = _pick(npg, (DECODE_PAGES_PER_STEP, 1))

    def pool_spec(gi):
        return pl.BlockSpec((None, None, page, heads, hd), lambda i, p, pt: (layer, pt[i, p * g + gi], 0, 0, 0))

    seq4 = lambda i, p, pt: (i, 0, 0, 0)
    return pl.pallas_call(
        functools.partial(_fox_decode_kernel, n_pages=g, t_valid=t_valid, ht=ht),
        grid_spec=pltpu.PrefetchScalarGridSpec(
            num_scalar_prefetch=1,
            grid=(b, npg // g),
            in_specs=[pl.BlockSpec((None, tq, heads, hd), seq4)]
            + [pool_spec(gi) for gi in range(g)] + [pool_spec(gi) for gi in range(g)]
            + [pl.BlockSpec((None, g, nt, 1, page * ht), lambda i, p, pt: (i, p, 0, 0, 0)),
               pl.BlockSpec((None, rows_n, heads, hd), seq4),
               pl.BlockSpec((None, rows_n, heads, hd), seq4),
               pl.BlockSpec((None, nt, 1, rows_n * ht), seq4)],
            out_specs=pl.BlockSpec((None, nt, tq * ht, hd), seq4),
            scratch_shapes=[pltpu.VMEM((nt, tq * ht, 1), f32), pltpu.VMEM((nt, tq * ht, 1), f32),
                            pltpu.VMEM((nt, tq * ht, hd), f32)],
        ),
        out_shape=jax.ShapeDtypeStruct((b, nt, tq * ht, hd), f32),
        compiler_params=_cparams(2),
        name="fox_decode_attention",
    )(page_table, q, *([pool_k] * g), *([pool_v] * g), r, k_new, v_new, a_new)


def _xattn_kernel(q_ref, k_ref, v_ref, o_ref, *, head_dim):
    for s in range(0, q_ref.shape[1], head_dim):
        q = q_ref[:, s:s + head_dim]
        kb = k_ref[:, s:s + head_dim].astype(bf16)
        vb = v_ref[:, s:s + head_dim].astype(bf16)
        sc = _dot_nt(q, kb)
        m = jnp.max(sc, axis=1, keepdims=True)
        p = jnp.exp(sc - m)
        l = jnp.sum(p, axis=1, keepdims=True)
        o_ref[:, s:s + head_dim] = (_dot(p.astype(bf16), vb) / l).astype(o_ref.dtype)


def mem_cross_attention(q, mk, mv, *, head_dim):
    b, t, w = q.shape
    mt = mk.shape[1]
    bq = _pick(t, (512, 256, 128))
    return pl.pallas_call(
        functools.partial(_xattn_kernel, head_dim=head_dim),
        grid=(b, t // bq),
        in_specs=[pl.BlockSpec((None, bq, w), lambda i, j: (i, j, 0)),
                  pl.BlockSpec((None, mt, w), lambda i, j: (i, 0, 0)),
                  pl.BlockSpec((None, mt, w), lambda i, j: (i, 0, 0))],
        out_specs=pl.BlockSpec((None, bq, w), lambda i, j: (i, j, 0)),
        out_shape=jax.ShapeDtypeStruct((b, t, w), bf16),
        compiler_params=_cparams(2),
        name="mem_cross_attention",
    )(q, mk, mv)


def _ffn(yp, ys, norm, w_in, w_out):
    f = w_in.shape[1] // 2
    bn = _pick(f, (256, 128))
    mp = yp.shape[0]
    bm = _pick(mp, (1024, 512, 256, 128))
    side = w_out if _side_ok(w_out.shape[0], f, bn, mp, bm) else None
    outs = _project(rmsnorm(yp, norm), rmsnorm(ys, norm), w_in, col0s=(0, f), n=f, bn=bn, bm=bm,
                    epilogue=_epi_swiglu, out_dtypes=(bf16,), side=side, name="ffn_in")
    hp, hs = outs[0], outs[1]
    wo = outs[2] if side is not None else w_out.astype(bf16)
    return (matmul_residual(hp, wo, yp, FFN_RES_SCALE), matmul_residual(hs, wo, ys, FFN_RES_SCALE))


def _out_proj(xp, xs, w, yp, ys, name):
    n = w.shape[1]
    return _project(xp, xs, w, col0s=(0,), n=n, bn=_pick(n, (512, 256, 128)),
                    epilogue=functools.partial(_epi_residual, scale=1.0), out_dtypes=(f32,),
                    extras_p=(yp,), extras_s=(ys,), name=name)


def _row(v):
    return v.reshape(1, -1).astype(f32)


def kernel(x_prompt, x_sample, state_mlstm_C, state_mlstm_n, state_mlstm_m, cache_fox_k, cache_fox_v, cache_fox_logf, cache_mem_k, cache_mem_v, page_table, mem_prompt, norm_ffn1, w_ffn1_in, w_ffn1_out, norm_mix, w_mlstm_in, b_mlstm_gate, mlstm_out_norm, w_mlstm_out, w_fox_in, b_fox_f, fox_q_norm, fox_k_norm, w_fox_out, norm_xattn, norm_mem, w_xq, w_xkv, xq_norm, xk_norm, w_xo, norm_ffn2, w_ffn2_in, w_ffn2_out):
    bp, seq, d = x_prompt.shape
    bs, dec_seq, _ = x_sample.shape
    depth = norm_ffn1.shape[0]
    n_mixers = 2
    m_heads = b_mlstm_gate.shape[-1] // 2
    m_dv = mlstm_out_norm.shape[-1]
    m_dqk = state_mlstm_n.shape[-1]
    m_main = 2 * m_heads * (m_dqk + m_dv)
    f_heads = b_fox_f.shape[-1]
    f_hd = fox_q_norm.shape[-1]
    f_width = f_heads * f_hd
    page = cache_fox_k.shape[2]
    x_heads, x_hd = cache_mem_k.shape[3], cache_mem_k.shape[4]
    x_width = x_heads * x_hd
    mem_tokens = mem_prompt.shape[1]
    rows = SAMPLE_ROWS
    ht = SUBLANES
    assert dec_seq <= rows and page == LANES

    yp = x_prompt.reshape(bp * seq, d)
    ys = jnp.pad(x_sample, ((0, 0), (0, rows - dec_seq), (0, 0))).reshape(bs * rows, d)
    memp = mem_prompt.reshape(bp * mem_tokens, d)
    chunk_p = _pick(seq, (MLSTM_PROMPT_CHUNK, 128, 64))

    c_p, n_p, m_p, c_s, n_s, m_s = [], [], [], [], [], []
    fk_p, fv_p, fl_p, fk_s, fv_s, fl_s = [], [], [], [], [], []
    mk_p, mv_p = [], []
    for layer in range(depth):
        j = layer // n_mixers
        yp, ys = _ffn(yp, ys, norm_ffn1[layer], w_ffn1_in[layer], w_ffn1_out[layer])

        hp, hs = rmsnorm(yp, norm_mix[layer]), rmsnorm(ys, norm_mix[layer])
        if layer % n_mixers == 0:
            w_in = w_mlstm_in[j]
            qkvo_p, qkvo_s = _project(hp, hs, w_in, col0s=(0,), n=m_main, bn=_pick(m_main, (512, 256, 128)),
                                      epilogue=_epi_plain, out_dtypes=(f32,), name="mlstm_in")
            w_gate = w_in[:, m_main:]
            zc = jnp.zeros((bp, m_heads, m_dqk, m_dv), f32)
            zn = jnp.zeros((bp, m_heads, m_dqk), f32)
            zm = jnp.zeros((bp, m_heads), f32)
            gp, c1, n1, m1 = mlstm(qkvo_p, matmul_narrow(hp, w_gate), b_mlstm_gate[j], mlstm_out_norm[j],
                                   zc, zn, zm, batch=bp, chunk=chunk_p)
            gs, c2, n2, m2 = mlstm(qkvo_s, matmul_narrow(hs, w_gate), b_mlstm_gate[j], mlstm_out_norm[j],
                                   state_mlstm_C[j], state_mlstm_n[j], state_mlstm_m[j],
                                   batch=bs, chunk=rows, t_valid=dec_seq)
            yp, ys = _out_proj(gp, gs, w_mlstm_out[j], yp, ys, "mlstm_out")
            c_p.append(c1); n_p.append(n1); m_p.append(m1)
            c_s.append(c2); n_s.append(n2); m_s.append(m2)
        else:
            w_in = w_fox_in[j]
            bn = _pick(f_width, (512, 256, 128))
            q_p, q_s = _project(hp, hs, w_in, col0s=(0,), n=f_width, bn=bn, name="fox_q",
                                epilogue=functools.partial(_epi_headnorm, scale=f_hd ** -0.5 * LOG2E, head_dim=f_hd),
                                out_dtypes=(bf16,), shared=(_row(fox_q_norm[j]),))
            k_p, kb_p, k_s, _ = _project(hp, hs, w_in, col0s=(f_width,), n=f_width, bn=bn, name="fox_k",
                                         epilogue=functools.partial(_epi_headnorm, scale=1.0, head_dim=f_hd),
                                         out_dtypes=(f32, bf16), shared=(_row(fox_k_norm[j]),))
            v_p, vb_p, v_s, _ = _project(hp, hs, w_in, col0s=(2 * f_width,), n=f_width, bn=bn, name="fox_v",
                                         epilogue=_epi_plain, out_dtypes=(f32, bf16))
            w_f = w_in[:, 3 * f_width:]
            lf_p = matmul_narrow(hp, w_f, b_fox_f[j], log_sigmoid=True)
            lf_s = matmul_narrow(hs, w_f, b_fox_f[j], log_sigmoid=True)
            c = cumsum_tokens_log2(lf_p.reshape(bp, seq, f_heads).transpose(0, 2, 1))
            o_p = fox_prompt_attention(q_p, kb_p, vb_p, c, batch=bp, heads=f_heads)
            fk_p.append(k_p.reshape(bp, seq // page, page, f_heads, f_hd))
            fv_p.append(v_p.reshape(bp, seq // page, page, f_heads, f_hd))
            fl_p.append(lf_p.reshape(bp, seq // page, page, f_heads))
            nt = f_heads // ht
            npg = page_table.shape[1]
            lf_t = lf_s.reshape(bs, rows, f_heads).transpose(0, 2, 1)
            a = cumsum_tokens_log2(jnp.pad(lf_t, ((0, 0), (0, 0), (0, LANES - rows))))[:, :, :rows]
            a = a.reshape(bs, nt, ht, rows).transpose(0, 1, 3, 2).reshape(bs, nt, 1, rows * ht)
            r = fox_past_decay_log2(cache_fox_logf, j, page_table)
            r = r.reshape(bs, nt, ht, npg, page).transpose(0, 3, 1, 4, 2).reshape(bs, npg, nt, 1, page * ht)
            q4 = q_s.reshape(bs, rows, f_heads, f_hd)[:, :dec_seq].astype(f32)
            k4 = k_s.reshape(bs, rows, f_heads, f_hd)
            v4 = v_s.reshape(bs, rows, f_heads, f_hd)
            o = fox_decode_attention(q4, cache_fox_k, cache_fox_v, j, page_table, r, k4, v4, a, t_valid=dec_seq)
            o = o.reshape(bs, nt, dec_seq, ht, f_hd).transpose(0, 2, 1, 3, 4).reshape(bs, dec_seq, f_width)
            o_s = jnp.pad(o, ((0, 0), (0, rows - dec_seq), (0, 0))).reshape(bs * rows, f_width).astype(bf16)
            yp, ys = _out_proj(o_p, o_s, w_fox_out[j], yp, ys, "fox_out")
            fk_s.append(k4[:, :dec_seq])
            fv_s.append(v4[:, :dec_seq])
            fl_s.append(lf_s.reshape(bs, rows, f_heads)[:, :dec_seq])

        mem_n = rmsnorm(memp, norm_mem[layer])
        mk, = _project(mem_n, None, w_xkv[layer], col0s=(0,), n=x_width, bn=x_width, name="mem_k",
                       epilogue=functools.partial(_epi_headnorm, scale=1.0, head_dim=x_hd),
                       out_dtypes=(f32,), shared=(_row(xk_norm[layer]),))
        mv, = _project(mem_n, None, w_xkv[layer], col0s=(x_width,), n=x_width, bn=x_width, name="mem_v",
                       epilogue=_epi_plain, out_dtypes=(f32,))
        mk_p.append(mk.reshape(bp, mem_tokens, x_heads, x_hd))
        mv_p.append(mv.reshape(bp, mem_tokens, x_heads, x_hd))
        xq_p, xq_s = _project(rmsnorm(yp, norm_xattn[layer]), rmsnorm(ys, norm_xattn[layer]), w_xq[layer],
                              col0s=(0,), n=x_width, bn=x_width, name="xattn_q",
                              epilogue=functools.partial(_epi_headnorm, scale=x_hd ** -0.5, head_dim=x_hd),
                              out_dtypes=(bf16,), shared=(_row(xq_norm[layer]),))
        xo_p = mem_cross_attention(xq_p.reshape(bp, seq, x_width), mk.reshape(bp, mem_tokens, x_width),
                                   mv.reshape(bp, mem_tokens, x_width), head_dim=x_hd)
        xo_s = mem_cross_attention(xq_s.reshape(bs, rows, x_width),
                                   cache_mem_k[layer].reshape(bs, mem_tokens, x_width),
                                   cache_mem_v[layer].reshape(bs, mem_tokens, x_width), head_dim=x_hd)
        yp, ys = _out_proj(xo_p.reshape(bp * seq, x_width), xo_s.reshape(bs * rows, x_width), w_xo[layer],
                           yp, ys, "xattn_out")

        yp, ys = _ffn(yp, ys, norm_ffn2[layer], w_ffn2_in[layer], w_ffn2_out[layer])

    yp = yp.reshape(bp, seq, d)
    ys = ys.reshape(bs, rows, d)[:, :dec_seq]
    return (yp, ys, jnp.stack(c_p), jnp.stack(n_p), jnp.stack(m_p), jnp.stack(fk_p), jnp.stack(fv_p),
            jnp.stack(fl_p), jnp.stack(mk_p), jnp.stack(mv_p), jnp.stack(c_s), jnp.stack(n_s), jnp.stack(m_s),
            jnp.stack(fk_s), jnp.stack(fv_s), jnp.stack(fl_s))
```

```python
import functools
import math

import jax
import jax.numpy as jnp
from jax import lax
from jax.experimental import pallas as pl
from jax.experimental.pallas import tpu as pltpu

RMS_EPS = 1e-6
FFN_RES_SCALE = 0.5
LOG2E = math.log2(math.e)
LANES = 128
SUBLANES = 8
SAMPLE_ROWS = 16
VMEM_LIMIT_BYTES = 56 * 1024 * 1024
MLSTM_PROMPT_CHUNK = 256
FOX_Q_BLOCK = 256
FOX_HEADS_PER_STEP = 4
DECODE_PAGES_PER_STEP = 4
DECAY_PAGES_PER_STEP = 16

f32 = jnp.float32
bf16 = jnp.bfloat16


def _cparams(n_axes):
    return pltpu.CompilerParams(
        dimension_semantics=("arbitrary",) * n_axes,
        vmem_limit_bytes=VMEM_LIMIT_BYTES)


def _dot(a, b):
    return jnp.dot(a, b, preferred_element_type=f32)


def _dot_nt(a, b):
    return lax.dot_general(a, b, (((1,), (1,)), ((), ())), preferred_element_type=f32)


def _dot_tn(a, b):
    return lax.dot_general(a, b, (((0,), (0,)), ((), ())), preferred_element_type=f32)


def _sigmoid(x):
    return 1.0 / (1.0 + jnp.exp(-x))


def _log_sigmoid(x):
    return -(jnp.maximum(-x, 0.0) + jnp.log(1.0 + jnp.exp(-jnp.abs(x))))


def _pick(n, candidates):
    for c in candidates:
        if n % c == 0:
            return c
    return n


def _rmsnorm_kernel(x_ref, g_ref, o_ref):
    x = x_ref[...]
    ms = jnp.mean(x * x, axis=-1, keepdims=True)
    o_ref[...] = (x * lax.rsqrt(ms + RMS_EPS) * g_ref[...]).astype(o_ref.dtype)


def rmsnorm(x, g):
    m, d = x.shape
    bm = _pick(m, (256, 128))
    return pl.pallas_call(
        _rmsnorm_kernel,
        grid=(m // bm,),
        in_specs=[pl.BlockSpec((bm, d), lambda i: (i, 0)),
                  pl.BlockSpec((1, d), lambda i: (0, 0))],
        out_specs=pl.BlockSpec((bm, d), lambda i: (i, 0)),
        out_shape=jax.ShapeDtypeStruct((m, d), bf16),
        compiler_params=_cparams(1),
        name="rmsnorm",
    )(x, g.reshape(1, d).astype(f32))


def _epi_plain(accs, shared, extras, outs):
    for o in outs:
        o[...] = accs[0].astype(o.dtype)


def _epi_residual(accs, shared, extras, outs, *, scale):
    acc = accs[0] if scale == 1.0 else scale * accs[0]
    outs[0][...] = extras[0][...] + acc


def _epi_swiglu(accs, shared, extras, outs):
    g, u = accs
    outs[0][...] = (g * _sigmoid(g) * u).astype(outs[0].dtype)


def _epi_headnorm(accs, shared, extras, outs, *, scale, head_dim):
    acc = accs[0]
    g = shared[0][...]
    for s in range(0, acc.shape[1], head_dim):
        a = acc[:, s:s + head_dim]
        ms = jnp.mean(a * a, axis=-1, keepdims=True)
        y = a * lax.rsqrt(ms + RMS_EPS) * g
        if scale != 1.0:
            y = y * scale
        for o in outs:
            o[:, s:s + head_dim] = y.astype(o.dtype)


def _proj_kernel(*refs, n_w, n_shared, n_extra, n_out, has_s, has_side, epilogue):
    it = iter(refs)
    xp = next(it)
    xs = next(it) if has_s else None
    ws = [next(it) for _ in range(n_w)]
    shared = [next(it) for _ in range(n_shared)]
    ext_p = [next(it) for _ in range(n_extra)]
    ext_s = [next(it) for _ in range(n_extra)] if has_s else None
    side_in = next(it) if has_side else None
    out_p = [next(it) for _ in range(n_out)]
    out_s = [next(it) for _ in range(n_out)] if has_s else None
    side_out = next(it) if has_side else None
    wscr = [next(it) for _ in range(n_w)]

    @pl.when(pl.program_id(1) == 0)
    def _():
        for w, s in zip(ws, wscr):
            s[...] = w[...].astype(bf16)
        if has_s:
            x = xs[...]
            epilogue([_dot(x, s[...]) for s in wscr], shared, ext_s, out_s)

    x = xp[...]
    epilogue([_dot(x, s[...]) for s in wscr], shared, ext_p, out_p)
    if has_side:
        side_out[...] = side_in[...].astype(bf16)


def _project(xp, xs, w, *, col0s, n, bn, epilogue, out_dtypes, shared=(), extras_p=(), extras_s=(),
             side=None, bm=None, name):
    w, w_layer = _layered(w)
    side, side_layer = _layered(side)
    mp, k = xp.shape
    bm = bm or _pick(mp, (1024, 512, 256, 128))
    assert mp % bm == 0 and n % bn == 0 and all(c % bn == 0 for c in col0s)
    nj, ni = n // bn, mp // bm
    has_s = xs is not None
    ms = xs.shape[0] if has_s else 0
    n_out = len(out_dtypes)

    in_specs = [pl.BlockSpec((bm, k), lambda j, i: (i, 0))]
    args = [xp]
    if has_s:
        in_specs.append(pl.BlockSpec((ms, k), lambda j, i: (0, 0)))
        args.append(xs)
    for c in col0s:
        if w_layer is None:
            in_specs.append(pl.BlockSpec((k, bn), lambda j, i, c0=c // bn: (0, j + c0)))
        else:
            in_specs.append(pl.BlockSpec((None, k, bn), lambda j, i, c0=c // bn: (w_layer, 0, j + c0)))
        args.append(w)
    for s in shared:
        in_specs.append(pl.BlockSpec(s.shape, lambda j, i: (0, 0)))
        args.append(s)
    for e in extras_p:
        in_specs.append(pl.BlockSpec((bm, bn), lambda j, i: (i, j)))
        args.append(e)
    if has_s:
        for e in extras_s:
            in_specs.append(pl.BlockSpec((ms, bn), lambda j, i: (0, j)))
            args.append(e)
    out_specs = [pl.BlockSpec((bm, bn), lambda j, i: (i, j)) for _ in out_dtypes]
    out_shape = [jax.ShapeDtypeStruct((mp, n), d) for d in out_dtypes]
    if has_s:
        out_specs += [pl.BlockSpec((ms, bn), lambda j, i: (0, j)) for _ in out_dtypes]
        out_shape += [jax.ShapeDtypeStruct((ms, n), d) for d in out_dtypes]
    has_side = side is not None
    if has_side:
        rows, cols = side.shape[-2:]
        slab = rows // (nj * ni)
        assert slab * nj * ni == rows and slab % 16 == 0
        if side_layer is None:
            in_specs.append(pl.BlockSpec((slab, cols), lambda j, i: (j * ni + i, 0)))
        else:
            in_specs.append(pl.BlockSpec((None, slab, cols), lambda j, i: (side_layer, j * ni + i, 0)))
        args.append(side)
        out_specs.append(pl.BlockSpec((slab, cols), lambda j, i: (j * ni + i, 0)))
        out_shape.append(jax.ShapeDtypeStruct((rows, cols), bf16))

    outs = pl.pallas_call(
        functools.partial(_proj_kernel, n_w=len(col0s), n_shared=len(shared), n_extra=len(extras_p),
                          n_out=n_out, has_s=has_s, has_side=has_side, epilogue=epilogue),
        grid=(nj, ni),
        in_specs=in_specs,
        out_specs=out_specs,
        out_shape=out_shape,
        scratch_shapes=[pltpu.VMEM((k, bn), bf16) for _ in col0s],
        compiler_params=_cparams(2),
        name=name,
    )(*args)
    return list(outs)


def _layered(w):
    return w if isinstance(w, tuple) else (w, None)


def _side_ok(rows, n, bn, mp, bm):
    steps = (n // bn) * (mp // bm)
    return rows % steps == 0 and (rows // steps) % 16 == 0


def _mm_residual_kernel(x_ref, w_ref, r_ref, o_ref, *, scale):
    acc = _dot(x_ref[...], w_ref[...])
    if scale != 1.0:
        acc = scale * acc
    o_ref[...] = r_ref[...] + acc


def matmul_residual(x, w, res, scale, *, bm=None, bn=None):
    m, k = x.shape
    n = w.shape[1]
    bm = bm or _pick(m, (512, 256, 128))
    bn = bn or _pick(n, (512, 256, 128))
    return pl.pallas_call(
        functools.partial(_mm_residual_kernel, scale=scale),
        grid=(m // bm, n // bn),
        in_specs=[pl.BlockSpec((bm, k), lambda i, j: (i, 0)),
                  pl.BlockSpec((k, bn), lambda i, j: (0, j)),
                  pl.BlockSpec((bm, bn), lambda i, j: (i, j))],
        out_specs=pl.BlockSpec((bm, bn), lambda i, j: (i, j)),
        out_shape=jax.ShapeDtypeStruct((m, n), f32),
        compiler_params=_cparams(2),
        name="matmul_residual",
    )(x, w, res)


def _mm_narrow_kernel(x_ref, w_ref, b_ref, o_ref, *, log_sigmoid):
    acc = _dot(x_ref[...], w_ref[...].astype(bf16))
    o_ref[...] = _log_sigmoid(acc + b_ref[...]) if log_sigmoid else acc


def matmul_narrow(x, w, b=None, *, log_sigmoid=False):
    m, k = x.shape
    n = w.shape[1]
    bm = _pick(m, (1024, 512, 256, 128))
    b = jnp.zeros((n,), f32) if b is None else b
    return pl.pallas_call(
        functools.partial(_mm_narrow_kernel, log_sigmoid=log_sigmoid),
        grid=(m // bm,),
        in_specs=[pl.BlockSpec((bm, k), lambda i: (i, 0)),
                  pl.BlockSpec((k, n), lambda i: (0, 0)),
                  pl.BlockSpec((1, n), lambda i: (0, 0))],
        out_specs=pl.BlockSpec((bm, n), lambda i: (i, 0)),
        out_shape=jax.ShapeDtypeStruct((m, n), f32),
        compiler_params=_cparams(1),
        name="matmul_narrow",
    )(x, w, b.reshape(1, n).astype(f32))


def _mlstm_kernel(q_ref, k_ref, v_ref, o_ref, gates_ref, bias_ref, onorm_ref,
                  c0_ref, n0_ref, m0_ref,
                  y_ref, c_ref, n_ref, m_ref, *, heads, t_valid):
    h = pl.program_id(1)
    ci = pl.program_id(2)
    L, dqk = q_ref.shape

    @pl.when(ci == 0)
    def _():
        c_ref[...] = c0_ref[...]
        n_ref[...] = n0_ref[...]
        m_ref[...] = m0_ref[...]

    row = lax.broadcasted_iota(jnp.int32, (L, L), 0)
    col = lax.broadcasted_iota(jnp.int32, (L, L), 1)
    eye = row == col
    causal = col <= row

    gates = gates_ref[...] + bias_ref[...]
    lane = lax.broadcasted_iota(jnp.int32, gates.shape, 1)
    ig_col = jnp.sum(jnp.where(lane == h, gates, 0.0), axis=1, keepdims=True)
    f_col = jnp.sum(jnp.where(lane == h + heads, gates, 0.0), axis=1, keepdims=True)
    lf_col = _log_sigmoid(f_col)
    if t_valid is not None:
        rvalid = lax.broadcasted_iota(jnp.int32, (L, 1), 0) < t_valid
        lf_col = jnp.where(rvalid, lf_col, 0.0)
        ig_col = jnp.where(rvalid, ig_col, -jnp.inf)

    b_row = jnp.sum(jnp.where(row <= col, lf_col, 0.0), axis=0, keepdims=True)
    lf_row = jnp.sum(jnp.where(eye, lf_col, 0.0), axis=0, keepdims=True)
    b_col = jnp.sum(jnp.where(causal, lf_row, 0.0), axis=1, keepdims=True)
    ig_row = jnp.sum(jnp.where(eye, ig_col, 0.0), axis=0, keepdims=True)
    u_row = ig_row - b_row
    u_col = ig_col - b_col

    m_prev = m_ref[:, 0:1]
    d = jnp.where(causal, b_col + u_row, -jnp.inf)
    inter = b_col + m_prev
    m_t = jnp.maximum(jnp.max(d, axis=1, keepdims=True), inter)
    p = jnp.exp(d - m_t)
    w_inter = jnp.exp(inter - m_t)

    q = q_ref[...].astype(bf16)
    kf = k_ref[...] * (dqk ** -0.5)
    kb = kf.astype(bf16)
    vb = v_ref[...].astype(bf16)
    c_prev = c_ref[...]
    n_prev = n_ref[...]

    s = _dot_nt(q, kb) * p
    num = w_inter * _dot(q, c_prev.astype(bf16)) + _dot(s.astype(bf16), vb)
    qn = jnp.sum(q_ref[...] * n_prev, axis=1, keepdims=True)
    den = w_inter * qn + jnp.sum(s, axis=1, keepdims=True)
    hid = num / jnp.maximum(jnp.abs(den), jnp.exp(-m_t))

    last = lax.broadcasted_iota(jnp.int32, (L, 1), 0) == L - 1
    m_new = jnp.sum(jnp.where(last, m_t, 0.0), axis=0, keepdims=True)
    b_last = jnp.sum(jnp.where(last, b_col, 0.0), axis=0, keepdims=True)
    w_c = jnp.exp(b_last + m_prev - m_new)
    w_s = jnp.exp(b_last + u_col - m_new)
    if t_valid is not None:
        w_s = jnp.where(rvalid, w_s, 0.0)
    kw = kf * w_s
    c_ref[...] = w_c * c_prev + _dot_tn(kw.astype(bf16), vb)
    n_ref[...] = w_c * n_prev + jnp.sum(kw, axis=0, keepdims=True)
    m_ref[...] = jnp.broadcast_to(m_new, m_ref.shape)

    ms = jnp.mean(hid * hid, axis=-1, keepdims=True)
    hn = hid * lax.rsqrt(ms + RMS_EPS) * onorm_ref[...]
    y_ref[...] = (hn * _sigmoid(o_ref[...])).astype(y_ref.dtype)


def mlstm(qkvo, gates, bias, out_norm, c0, n0, m0, *, batch, chunk, t_valid=None):
    bt = qkvo.shape[0]
    t = bt // batch
    heads, dqk, dv = c0.shape[1:]
    nc = t // chunk
    assert t % chunk == 0 and dv % dqk == 0
    kq = heads
    kv = 2 * heads * dqk // dv
    ko = kv + heads
    n0r = n0.reshape(batch * heads, 1, dqk).astype(f32)
    m0r = jnp.broadcast_to(m0.reshape(batch * heads, 1, 1).astype(f32), (batch * heads, 1, LANES))
    y, c, n, m = pl.pallas_call(
        functools.partial(_mlstm_kernel, heads=heads, t_valid=t_valid),
        grid=(batch, heads, nc),
        in_specs=[
            pl.BlockSpec((chunk, dqk), lambda b, h, c: (b * nc + c, h)),
            pl.BlockSpec((chunk, dqk), lambda b, h, c: (b * nc + c, kq + h)),
            pl.BlockSpec((chunk, dv), lambda b, h, c: (b * nc + c, kv + h)),
            pl.BlockSpec((chunk, dv), lambda b, h, c: (b * nc + c, ko + h)),
            pl.BlockSpec((chunk, 2 * heads), lambda b, h, c: (b * nc + c, 0)),
            pl.BlockSpec((1, 2 * heads), lambda b, h, c: (0, 0)),
            pl.BlockSpec((None, 1, dv), lambda b, h, c: (h, 0, 0)),
            pl.BlockSpec((None, None, dqk, dv), lambda b, h, c: (b, h, 0, 0)),
            pl.BlockSpec((None, 1, dqk), lambda b, h, c: (b * heads + h, 0, 0)),
            pl.BlockSpec((None, 1, LANES), lambda b, h, c: (b * heads + h, 0, 0)),
        ],
        out_specs=[
            pl.BlockSpec((chunk, dv), lambda b, h, c: (b * nc + c, h)),
            pl.BlockSpec((None, None, dqk, dv), lambda b, h, c: (b, h, 0, 0)),
            pl.BlockSpec((None, 1, dqk), lambda b, h, c: (b * heads + h, 0, 0)),
            pl.BlockSpec((None, 1, LANES), lambda b, h, c: (b * heads + h, 0, 0)),
        ],
        out_shape=[
            jax.ShapeDtypeStruct((bt, heads * dv), bf16),
            jax.ShapeDtypeStruct((batch, heads, dqk, dv), f32),
            jax.ShapeDtypeStruct((batch * heads, 1, dqk), f32),
            jax.ShapeDtypeStruct((batch * heads, 1, LANES), f32),
        ],
        compiler_params=_cparams(3),
        name="mlstm",
    )(qkvo, qkvo, qkvo, qkvo, gates, bias.reshape(1, 2 * heads).astype(f32),
      out_norm.reshape(heads, 1, dv).astype(f32), c0.astype(f32), n0r, m0r)
    return y, c, n.reshape(batch, heads, dqk), m[:, 0, 0].reshape(batch, heads)


def _lane_cumsum(x, reverse=False):
    lane = lax.broadcasted_iota(jnp.int32, x.shape, x.ndim - 1)
    sh = 1
    while sh < LANES:
        if reverse:
            x = x + jnp.where(lane + sh < LANES, pltpu.roll(x, LANES - sh, x.ndim - 1), 0.0)
        else:
            x = x + jnp.where(lane >= sh, pltpu.roll(x, sh, x.ndim - 1), 0.0)
        sh *= 2
    return x


def _cumsum_kernel(x_ref, o_ref):
    hh, t = x_ref.shape
    carry = jnp.zeros((hh, 1), f32)
    for c in range(t // LANES):
        y = _lane_cumsum(x_ref[:, c * LANES:(c + 1) * LANES]) + carry
        o_ref[:, c * LANES:(c + 1) * LANES] = y * LOG2E
        carry = y[:, LANES - 1:LANES]


def cumsum_tokens_log2(x):
    b, hh, t = x.shape
    return pl.pallas_call(
        _cumsum_kernel,
        grid=(b,),
        in_specs=[pl.BlockSpec((None, hh, t), lambda i: (i, 0, 0))],
        out_specs=pl.BlockSpec((None, hh, t), lambda i: (i, 0, 0)),
        out_shape=jax.ShapeDtypeStruct((b, hh, t), f32),
        compiler_params=_cparams(1),
        name="cumsum_tokens",
    )(x)


def _fox_prompt_kernel(q_ref, k_ref, v_ref, c_ref, o_ref, *, blk, hd, n_heads):
    qi = pl.program_id(2)

    def attend(blocks, carry):
        out = []
        for hh in range(n_heads):
            m, l, acc = carry[hh]
            cols = slice(hh * hd, (hh + 1) * hd)
            q = q_ref[:, cols]
            scores = []
            for j, mask in blocks:
                start = pl.multiple_of(j * blk, blk)
                s = _dot_nt(q, k_ref[pl.ds(start, blk), cols]) - c_ref[hh, j]
                if mask:
                    row = lax.broadcasted_iota(jnp.int32, (blk, blk), 0)
                    col = lax.broadcasted_iota(jnp.int32, (blk, blk), 1)
                    s = jnp.where(col <= row, s, -jnp.inf)
                scores.append(s)
            m_new = m
            for s in scores:
                m_new = jnp.maximum(m_new, jnp.max(s, axis=1, keepdims=True))
            alpha = jnp.exp2(m - m_new)
            l = alpha * l
            acc = alpha * acc
            for (j, _), s in zip(blocks, scores):
                start = pl.multiple_of(j * blk, blk)
                p = jnp.exp2(s - m_new)
                l = l + jnp.sum(p, axis=1, keepdims=True)
                acc = acc + _dot(p.astype(bf16), v_ref[pl.ds(start, blk), cols])
            out.append((m_new, l, acc))
        return tuple(out)

    init = tuple((jnp.full((blk, 1), -jnp.inf, f32), jnp.zeros((blk, 1), f32), jnp.zeros((blk, hd), f32))
                 for _ in range(n_heads))
    carry = lax.fori_loop(0, qi // 2, lambda j, c: attend([(2 * j, False), (2 * j + 1, False)], c), init)
    carry = lax.cond(qi % 2 == 1,
                     lambda c: attend([(qi - 1, False), (qi, True)], c),
                     lambda c: attend([(qi, True)], c), carry)
    for hh in range(n_heads):
        _, l, acc = carry[hh]
        o_ref[:, hh * hd:(hh + 1) * hd] = (acc / l).astype(o_ref.dtype)


def fox_prompt_attention(q, k, v, c, *, batch, heads):
    bt, width = q.shape
    t = bt // batch
    hd = width // heads
    blk = _pick(t, (FOX_Q_BLOCK, LANES))
    nq = t // blk
    nh = FOX_HEADS_PER_STEP if heads % FOX_HEADS_PER_STEP == 0 else 1
    c5 = c.reshape(batch, heads, nq, 1, blk)
    return pl.pallas_call(
        functools.partial(_fox_prompt_kernel, blk=blk, hd=hd, n_heads=nh),
        grid=(batch, heads // nh, nq),
        in_specs=[
            pl.BlockSpec((blk, nh * hd), lambda b, h, i: (b * nq + i, h)),
            pl.BlockSpec((t, nh * hd), lambda b, h, i: (b, h)),
            pl.BlockSpec((t, nh * hd), lambda b, h, i: (b, h)),
            pl.BlockSpec((None, nh, nq, 1, blk), lambda b, h, i: (b, h, 0, 0, 0)),
        ],
        out_specs=pl.BlockSpec((blk, nh * hd), lambda b, h, i: (b * nq + i, h)),
        out_shape=jax.ShapeDtypeStruct((bt, width), bf16),
        compiler_params=_cparams(3),
        name="fox_prompt_attention",
    )(q, k, v, c5)


def _fox_decay_kernel(pt_ref, *refs, n_pages):
    lf_refs = refs[:n_pages]
    r_ref, pad_ref, carry_ref = refs[n_pages:]
    page, hh = lf_refs[0].shape

    @pl.when(pl.program_id(1) == 0)
    def _():
        carry_ref[...] = jnp.zeros_like(carry_ref)
        pad_ref[...] = jnp.zeros_like(pad_ref)

    for g in range(n_pages):
        pad_ref[g, :, 0:hh] = lf_refs[g][...]
    x_all = jnp.concatenate([pad_ref[g].T[0:hh, :] for g in range(n_pages)], axis=0)
    y_all = _lane_cumsum(x_all, reverse=True)
    carry = carry_ref[...]
    for g in reversed(range(n_pages)):
        x = x_all[g * hh:(g + 1) * hh]
        y = y_all[g * hh:(g + 1) * hh]
        r_ref[:, g * page:(g + 1) * page] = ((y - x) + carry) * LOG2E
        carry = carry + y[:, 0:1]
    carry_ref[...] = carry


def fox_past_decay_log2(pool_lf, layer, page_table):
    _, _, page, hh = pool_lf.shape
    b, npg = page_table.shape
    g = _pick(npg, (DECAY_PAGES_PER_STEP, 8, 4, 2, 1))
    ns = npg // g
    assert page == LANES

    def lf_spec(gi):
        return pl.BlockSpec((None, None, page, hh),
                            lambda i, s, pt: (layer, pt[i, (ns - 1 - s) * g + gi], 0, 0))

    return pl.pallas_call(
        functools.partial(_fox_decay_kernel, n_pages=g),
        grid_spec=pltpu.PrefetchScalarGridSpec(
            num_scalar_prefetch=1,
            grid=(b, ns),
            in_specs=[lf_spec(gi) for gi in range(g)],
            out_specs=pl.BlockSpec((None, hh, g * page), lambda i, s, pt: (i, 0, ns - 1 - s)),
            scratch_shapes=[pltpu.VMEM((g, page, LANES), f32), pltpu.VMEM((hh, page), f32)],
        ),
        out_shape=jax.ShapeDtypeStruct((b, hh, npg * page), f32),
        compiler_params=_cparams(2),
        name="fox_past_decay",
    )(page_table, *([pool_lf] * g))


def _fox_decode_kernel(pt_ref, q_ref, *refs, n_pages, t_valid, ht):
    kps = refs[:n_pages]
    vps = refs[n_pages:2 * n_pages]
    r_ref, kn_ref, vn_ref, a_ref, o_ref, m_ref, l_ref, acc_ref = refs[2 * n_pages:]
    p = pl.program_id(1)
    last = pl.num_programs(1) - 1
    tq, heads, hd = q_ref.shape
    nt = heads // ht
    rq = tq * ht
    page = kps[0].shape[0]
    rows_n = kn_ref.shape[0]

    @pl.when(p == 0)
    def _():
        m_ref[...] = jnp.full_like(m_ref, -jnp.inf)
        l_ref[...] = jnp.zeros_like(l_ref)
        acc_ref[...] = jnp.zeros_like(acc_ref)

    def tile(ref, a):
        x = ref[:, a * ht:(a + 1) * ht, :]
        return x.reshape(x.shape[0] * ht, hd).astype(bf16)

    def attend(a, qa, blocks):
        scores = [_dot_nt(qa, kb) + bias for kb, _, bias in blocks]
        m_old = m_ref[a]
        m_new = m_old
        for s in scores:
            m_new = jnp.maximum(m_new, jnp.max(s, axis=1, keepdims=True))
        alpha = jnp.exp2(m_old - m_new)
        l = alpha * l_ref[a]
        acc = alpha * acc_ref[a]
        for (_, vb, _), s in zip(blocks, scores):
            pr = jnp.exp2(s - m_new)
            l = l + jnp.sum(pr, axis=1, keepdims=True)
            acc = acc + _dot(pr.astype(bf16), vb)
        l_ref[a] = l
        acc_ref[a] = acc
        m_ref[a] = m_new

    row = lax.broadcasted_iota(jnp.int32, (rq, page * ht), 0)
    col = lax.broadcasted_iota(jnp.int32, (rq, page * ht), 1)
    head_mask = jnp.where((col & (ht - 1)) == (row & (ht - 1)), 0.0, -jnp.inf)
    for a in range(nt):
        attend(a, tile(q_ref, a),
               [(tile(kps[g], a), tile(vps[g], a), r_ref[g, a] + head_mask) for g in range(n_pages)])

    @pl.when(p == last)
    def _():
        shift = ht.bit_length() - 1
        rown = lax.broadcasted_iota(jnp.int32, (rq, rows_n * ht), 0)
        coln = lax.broadcasted_iota(jnp.int32, (rq, rows_n * ht), 1)
        tk = coln >> shift
        allowed = ((coln & (ht - 1)) == (rown & (ht - 1))) & (tk <= (rown >> shift)) & (tk < t_valid)
        for a in range(nt):
            bias = jnp.where(allowed, -a_ref[a], -jnp.inf)
            attend(a, tile(q_ref, a), [(tile(kn_ref, a), tile(vn_ref, a), bias)])
            o_ref[a] = acc_ref[a] / l_ref[a]


def fox_decode_attention(q, pool_k, pool_v, layer, page_table, r, k_new, v_new, a_new, *, t_valid):
    b, tq, heads, hd = q.shape
    page = pool_k.shape[2]
    npg = page_table.shape[1]
    rows_n = k_new.shape[1]
    ht = SUBLANES
    assert heads % ht == 0
    nt = heads // ht
    g = _pick(npg, (DECODE_PAGES_PER_STEP, 1))

    def pool_spec(gi):
        return pl.BlockSpec((None, None, page, heads, hd), lambda i, p, pt: (layer, pt[i, p * g + gi], 0, 0, 0))

    seq4 = lambda i, p, pt: (i, 0, 0, 0)
    return pl.pallas_call(
        functools.partial(_fox_decode_kernel, n_pages=g, t_valid=t_valid, ht=ht),
        grid_spec=pltpu.PrefetchScalarGridSpec(
            num_scalar_prefetch=1,
            grid=(b, npg // g),
            in_specs=[pl.BlockSpec((None, tq, heads, hd), seq4)]
            + [pool_spec(gi) for gi in range(g)] + [pool_spec(gi) for gi in range(g)]
            + [pl.BlockSpec((None, g, nt, 1, page * ht), lambda i, p, pt: (i, p, 0, 0, 0)),
               pl.BlockSpec((None, rows_n, heads, hd), seq4),
               pl.BlockSpec((None, rows_n, heads, hd), seq4),
               pl.BlockSpec((None, nt, 1, rows_n * ht), seq4)],
            out_specs=pl.BlockSpec((None, nt, tq * ht, hd), seq4),
            scratch_shapes=[pltpu.VMEM((nt, tq * ht, 1), f32), pltpu.VMEM((nt, tq * ht, 1), f32),
                            pltpu.VMEM((nt, tq * ht, hd), f32)],
        ),
        out_shape=jax.ShapeDtypeStruct((b, nt, tq * ht, hd), f32),
        compiler_params=_cparams(2),
        name="fox_decode_attention",
    )(page_table, q, *([pool_k] * g), *([pool_v] * g), r, k_new, v_new, a_new)


def _xattn_kernel(q_ref, k_ref, v_ref, o_ref, *, head_dim):
    for s in range(0, q_ref.shape[1], head_dim):
        q = q_ref[:, s:s + head_dim]
        kb = k_ref[:, s:s + head_dim].astype(bf16)
        vb = v_ref[:, s:s + head_dim].astype(bf16)
        sc = _dot_nt(q, kb)
        m = jnp.max(sc, axis=1, keepdims=True)
        p = jnp.exp(sc - m)
        l = jnp.sum(p, axis=1, keepdims=True)
        o_ref[:, s:s + head_dim] = (_dot(p.astype(bf16), vb) / l).astype(o_ref.dtype)


def mem_cross_attention(q, mk, mv, *, head_dim):
    b, t, w = q.shape
    mt = mk.shape[1]
    bq = _pick(t, (512, 256, 128))
    return pl.pallas_call(
        functools.partial(_xattn_kernel, head_dim=head_dim),
        grid=(b, t // bq),
        in_specs=[pl.BlockSpec((None, bq, w), lambda i, j: (i, j, 0)),
                  pl.BlockSpec((None, mt, w), lambda i, j: (i, 0, 0)),
                  pl.BlockSpec((None, mt, w), lambda i, j: (i, 0, 0))],
        out_specs=pl.BlockSpec((None, bq, w), lambda i, j: (i, j, 0)),
        out_shape=jax.ShapeDtypeStruct((b, t, w), bf16),
        compiler_params=_cparams(2),
        name="mem_cross_attention",
    )(q, mk, mv)


def _ffn(yp, ys, norm, w_in, w_out, layer):
    f = w_in.shape[-1] // 2
    bn = _pick(f, (256, 128))
    mp = yp.shape[0]
    bm = _pick(mp, (1024, 512, 256, 128))
    side = (w_out, layer) if _side_ok(w_out.shape[-2], f, bn, mp, bm) else None
    outs = _project(rmsnorm(yp, norm), rmsnorm(ys, norm), (w_in, layer), col0s=(0, f), n=f, bn=bn, bm=bm,
                    epilogue=_epi_swiglu, out_dtypes=(bf16,), side=side, name="ffn_in")
    hp, hs = outs[0], outs[1]
    wo = outs[2] if side is not None else w_out[layer].astype(bf16)
    return (matmul_residual(hp, wo, yp, FFN_RES_SCALE), matmul_residual(hs, wo, ys, FFN_RES_SCALE))


def _out_proj(xp, xs, w, yp, ys, name):
    n = _layered(w)[0].shape[-1]
    return _project(xp, xs, w, col0s=(0,), n=n, bn=_pick(n, (512, 256, 128)),
                    epilogue=functools.partial(_epi_residual, scale=1.0), out_dtypes=(f32,),
                    extras_p=(yp,), extras_s=(ys,), name=name)


def _row(v):
    return v.reshape(1, -1).astype(f32)


def kernel(x_prompt, x_sample, state_mlstm_C, state_mlstm_n, state_mlstm_m, cache_fox_k, cache_fox_v, cache_fox_logf, cache_mem_k, cache_mem_v, page_table, mem_prompt, norm_ffn1, w_ffn1_in, w_ffn1_out, norm_mix, w_mlstm_in, b_mlstm_gate, mlstm_out_norm, w_mlstm_out, w_fox_in, b_fox_f, fox_q_norm, fox_k_norm, w_fox_out, norm_xattn, norm_mem, w_xq, w_xkv, xq_norm, xk_norm, w_xo, norm_ffn2, w_ffn2_in, w_ffn2_out):
    bp, seq, d = x_prompt.shape
    bs, dec_seq, _ = x_sample.shape
    depth = norm_ffn1.shape[0]
    n_mixers = 2
    m_heads = b_mlstm_gate.shape[-1] // 2
    m_dv = mlstm_out_norm.shape[-1]
    m_dqk = state_mlstm_n.shape[-1]
    m_main = 2 * m_heads * (m_dqk + m_dv)
    f_heads = b_fox_f.shape[-1]
    f_hd = fox_q_norm.shape[-1]
    f_width = f_heads * f_hd
    page = cache_fox_k.shape[2]
    x_heads, x_hd = cache_mem_k.shape[3], cache_mem_k.shape[4]
    x_width = x_heads * x_hd
    mem_tokens = mem_prompt.shape[1]
    rows = SAMPLE_ROWS
    ht = SUBLANES
    assert dec_seq <= rows and page == LANES

    yp = x_prompt.reshape(bp * seq, d)
    ys = jnp.pad(x_sample, ((0, 0), (0, rows - dec_seq), (0, 0))).reshape(bs * rows, d)
    memp = mem_prompt.reshape(bp * mem_tokens, d)
    chunk_p = _pick(seq, (MLSTM_PROMPT_CHUNK, 128, 64))

    c_p, n_p, m_p, c_s, n_s, m_s = [], [], [], [], [], []
    fk_p, fv_p, fl_p, fk_s, fv_s, fl_s = [], [], [], [], [], []
    mk_p, mv_p = [], []
    for layer in range(depth):
        j = layer // n_mixers
        yp, ys = _ffn(yp, ys, norm_ffn1[layer], w_ffn1_in, w_ffn1_out, layer)

        hp, hs = rmsnorm(yp, norm_mix[layer]), rmsnorm(ys, norm_mix[layer])
        if layer % n_mixers == 0:
            w_in = (w_mlstm_in, j)
            qkvo_p, qkvo_s = _project(hp, hs, w_in, col0s=(0,), n=m_main, bn=_pick(m_main, (512, 256, 128)),
                                      epilogue=_epi_plain, out_dtypes=(f32,), name="mlstm_in")
            w_gate = w_mlstm_in[j, :, m_main:]
            zc = jnp.zeros((bp, m_heads, m_dqk, m_dv), f32)
            zn = jnp.zeros((bp, m_heads, m_dqk), f32)
            zm = jnp.zeros((bp, m_heads), f32)
            gp, c1, n1, m1 = mlstm(qkvo_p, matmul_narrow(hp, w_gate), b_mlstm_gate[j], mlstm_out_norm[j],
                                   zc, zn, zm, batch=bp, chunk=chunk_p)
            gs, c2, n2, m2 = mlstm(qkvo_s, matmul_narrow(hs, w_gate), b_mlstm_gate[j], mlstm_out_norm[j],
                                   state_mlstm_C[j], state_mlstm_n[j], state_mlstm_m[j],
                                   batch=bs, chunk=rows, t_valid=dec_seq)
            yp, ys = _out_proj(gp, gs, (w_mlstm_out, j), yp, ys, "mlstm_out")
            c_p.append(c1); n_p.append(n1); m_p.append(m1)
            c_s.append(c2); n_s.append(n2); m_s.append(m2)
        else:
            w_in = (w_fox_in, j)
            bn = _pick(f_width, (512, 256, 128))
            q_p, q_s = _project(hp, hs, w_in, col0s=(0,), n=f_width, bn=bn, name="fox_q",
                                epilogue=functools.partial(_epi_headnorm, scale=f_hd ** -0.5 * LOG2E, head_dim=f_hd),
                                out_dtypes=(bf16,), shared=(_row(fox_q_norm[j]),))
            k_p, kb_p, k_s, _ = _project(hp, hs, w_in, col0s=(f_width,), n=f_width, bn=bn, name="fox_k",
                                         epilogue=functools.partial(_epi_headnorm, scale=1.0, head_dim=f_hd),
                                         out_dtypes=(f32, bf16), shared=(_row(fox_k_norm[j]),))
            v_p, vb_p, v_s, _ = _project(hp, hs, w_in, col0s=(2 * f_width,), n=f_width, bn=bn, name="fox_v",
                                         epilogue=_epi_plain, out_dtypes=(f32, bf16))
            w_f = w_fox_in[j, :, 3 * f_width:]
            lf_p = matmul_narrow(hp, w_f, b_fox_f[j], log_sigmoid=True)
            lf_s = matmul_narrow(hs, w_f, b_fox_f[j], log_sigmoid=True)
            c = cumsum_tokens_log2(lf_p.reshape(bp, seq, f_heads).transpose(0, 2, 1))
            o_p = fox_prompt_attention(q_p, kb_p, vb_p, c, batch=bp, heads=f_heads)
            fk_p.append(k_p.reshape(bp, seq // page, page, f_heads, f_hd))
            fv_p.append(v_p.reshape(bp, seq // page, page, f_heads, f_hd))
            fl_p.append(lf_p.reshape(bp, seq // page, page, f_heads))
            nt = f_heads // ht
            npg = page_table.shape[1]
            lf_t = lf_s.reshape(bs, rows, f_heads).transpose(0, 2, 1)
            a = cumsum_tokens_log2(jnp.pad(lf_t, ((0, 0), (0, 0), (0, LANES - rows))))[:, :, :rows]
            a = a.reshape(bs, nt, ht, rows).transpose(0, 1, 3, 2).reshape(bs, nt, 1, rows * ht)
            r = fox_past_decay_log2(cache_fox_logf, j, page_table)
            r = r.reshape(bs, nt, ht, npg, page).transpose(0, 3, 1, 4, 2).reshape(bs, npg, nt, 1, page * ht)
            q4 = q_s.reshape(bs, rows, f_heads, f_hd)[:, :dec_seq].astype(f32)
            k4 = k_s.reshape(bs, rows, f_heads, f_hd)
            v4 = v_s.reshape(bs, rows, f_heads, f_hd)
            o = fox_decode_attention(q4, cache_fox_k, cache_fox_v, j, page_table, r, k4, v4, a, t_valid=dec_seq)
            o = o.reshape(bs, nt, dec_seq, ht, f_hd).transpose(0, 2, 1, 3, 4).reshape(bs, dec_seq, f_width)
            o_s = jnp.pad(o, ((0, 0), (0, rows - dec_seq), (0, 0))).reshape(bs * rows, f_width).astype(bf16)
            yp, ys = _out_proj(o_p, o_s, (w_fox_out, j), yp, ys, "fox_out")
            fk_s.append(k4[:, :dec_seq])
            fv_s.append(v4[:, :dec_seq])
            fl_s.append(lf_s.reshape(bs, rows, f_heads)[:, :dec_seq])

        mem_n = rmsnorm(memp, norm_mem[layer])
        mk, = _project(mem_n, None, (w_xkv, layer), col0s=(0,), n=x_width, bn=x_width, name="mem_k",
                       epilogue=functools.partial(_epi_headnorm, scale=1.0, head_dim=x_hd),
                       out_dtypes=(f32,), shared=(_row(xk_norm[layer]),))
        mv, = _project(mem_n, None, (w_xkv, layer), col0s=(x_width,), n=x_width, bn=x_width, name="mem_v",
                       epilogue=_epi_plain, out_dtypes=(f32,))
        mk_p.append(mk.reshape(bp, mem_tokens, x_heads, x_hd))
        mv_p.append(mv.reshape(bp, mem_tokens, x_heads, x_hd))
        xq_p, xq_s = _project(rmsnorm(yp, norm_xattn[layer]), rmsnorm(ys, norm_xattn[layer]), (w_xq, layer),
                              col0s=(0,), n=x_width, bn=x_width, name="xattn_q",
                              epilogue=functools.partial(_epi_headnorm, scale=x_hd ** -0.5, head_dim=x_hd),
                              out_dtypes=(bf16,), shared=(_row(xq_norm[layer]),))
        xo_p = mem_cross_attention(xq_p.reshape(bp, seq, x_width), mk.reshape(bp, mem_tokens, x_width),
                                   mv.reshape(bp, mem_tokens, x_width), head_dim=x_hd)
        xo_s = mem_cross_attention(xq_s.reshape(bs, rows, x_width),
                                   cache_mem_k[layer].reshape(bs, mem_tokens, x_width),
                                   cache_mem_v[layer].reshape(bs, mem_tokens, x_width), head_dim=x_hd)
        yp, ys = _out_proj(xo_p.reshape(bp * seq, x_width), xo_s.reshape(bs * rows, x_width), (w_xo, layer),
                           yp, ys, "xattn_out")

        yp, ys = _ffn(yp, ys, norm_ffn2[layer], w_ffn2_in, w_ffn2_out, layer)

    yp = yp.reshape(bp, seq, d)
    ys = ys.reshape(bs, rows, d)[:, :dec_seq]
    return (yp, ys, jnp.stack(c_p), jnp.stack(n_p), jnp.stack(m_p), jnp.stack(fk_p), jnp.stack(fv_p),
            jnp.stack(fl_p), jnp.stack(mk_p), jnp.stack(mv_p), jnp.stack(c_s), jnp.stack(n_s), jnp.stack(m_s),
            jnp.stack(fk_s), jnp.stack(fv_s), jnp.stack(fl_s))
```

```python
import functools
import math

import jax
import jax.numpy as jnp
from jax import lax
from jax.experimental import pallas as pl
from jax.experimental.pallas import tpu as pltpu

RMS_EPS = 1e-6
FFN_RES_SCALE = 0.5
LOG2E = math.log2(math.e)
LANES = 128
SUBLANES = 8
SAMPLE_ROWS = 16
VMEM_LIMIT_BYTES = 56 * 1024 * 1024
MLSTM_PROMPT_CHUNK = 256
FOX_Q_BLOCK = 256
FOX_HEADS_PER_STEP = 4
DECODE_PAGES_PER_STEP = 4
DECAY_PAGES_PER_STEP = 16

f32 = jnp.float32
bf16 = jnp.bfloat16


def _cparams(n_axes):
    return pltpu.CompilerParams(
        dimension_semantics=("arbitrary",) * n_axes,
        vmem_limit_bytes=VMEM_LIMIT_BYTES)


def _dot(a, b):
    return jnp.dot(a, b, preferred_element_type=f32)


def _dot_nt(a, b):
    return lax.dot_general(a, b, (((1,), (1,)), ((), ())), preferred_element_type=f32)


def _dot_tn(a, b):
    return lax.dot_general(a, b, (((0,), (0,)), ((), ())), preferred_element_type=f32)


def _sigmoid(x):
    return 1.0 / (1.0 + jnp.exp(-x))


def _log_sigmoid(x):
    return -(jnp.maximum(-x, 0.0) + jnp.log(1.0 + jnp.exp(-jnp.abs(x))))


def _pick(n, candidates):
    for c in candidates:
        if n % c == 0:
            return c
    return n


def _rmsnorm_kernel(x_ref, g_ref, o_ref):
    x = x_ref[...]
    ms = jnp.mean(x * x, axis=-1, keepdims=True)
    o_ref[...] = (x * lax.rsqrt(ms + RMS_EPS) * g_ref[...]).astype(o_ref.dtype)


def rmsnorm(x, g):
    m, d = x.shape
    bm = _pick(m, (256, 128))
    return pl.pallas_call(
        _rmsnorm_kernel,
        grid=(m // bm,),
        in_specs=[pl.BlockSpec((bm, d), lambda i: (i, 0)),
                  pl.BlockSpec((1, d), lambda i: (0, 0))],
        out_specs=pl.BlockSpec((bm, d), lambda i: (i, 0)),
        out_shape=jax.ShapeDtypeStruct((m, d), bf16),
        compiler_params=_cparams(1),
        name="rmsnorm",
    )(x, g.reshape(1, d).astype(f32))


def _epi_plain(accs, shared, extras, outs):
    for o in outs:
        o[...] = accs[0].astype(o.dtype)


def _epi_residual(accs, shared, extras, outs, *, scale):
    acc = accs[0] if scale == 1.0 else scale * accs[0]
    outs[0][...] = extras[0][...] + acc


def _epi_swiglu(accs, shared, extras, outs):
    g, u = accs
    outs[0][...] = (g * _sigmoid(g) * u).astype(outs[0].dtype)


def _epi_headnorm(accs, shared, extras, outs, *, scale, head_dim):
    acc = accs[0]
    g = shared[0][...]
    for s in range(0, acc.shape[1], head_dim):
        a = acc[:, s:s + head_dim]
        ms = jnp.mean(a * a, axis=-1, keepdims=True)
        y = a * lax.rsqrt(ms + RMS_EPS) * g
        if scale != 1.0:
            y = y * scale
        for o in outs:
            o[:, s:s + head_dim] = y.astype(o.dtype)


def _proj_kernel(*refs, n_w, n_shared, n_extra, n_out, has_s, has_side, epilogue, w_rows_are_outputs):
    dot = _dot_nt if w_rows_are_outputs else _dot
    it = iter(refs)
    xp = next(it)
    xs = next(it) if has_s else None
    ws = [next(it) for _ in range(n_w)]
    shared = [next(it) for _ in range(n_shared)]
    ext_p = [next(it) for _ in range(n_extra)]
    ext_s = [next(it) for _ in range(n_extra)] if has_s else None
    side_in = next(it) if has_side else None
    out_p = [next(it) for _ in range(n_out)]
    out_s = [next(it) for _ in range(n_out)] if has_s else None
    side_out = next(it) if has_side else None
    wscr = [next(it) for _ in range(n_w)]

    @pl.when(pl.program_id(1) == 0)
    def _():
        for w, s in zip(ws, wscr):
            s[...] = w[...].astype(bf16)
        if has_s:
            x = xs[...]
            epilogue([dot(x, s[...]) for s in wscr], shared, ext_s, out_s)

    x = xp[...]
    epilogue([dot(x, s[...]) for s in wscr], shared, ext_p, out_p)
    if has_side:
        side_out[...] = side_in[...].astype(bf16)


def _project(xp, xs, w, *, col0s, n, bn, epilogue, out_dtypes, shared=(), extras_p=(), extras_s=(),
             side=None, bm=None, w_t=False, name):
    w, w_layer = _layered(w)
    side, side_layer = _layered(side)
    mp, k = xp.shape
    bm = bm or _pick(mp, (1024, 512, 256, 128))
    assert mp % bm == 0 and n % bn == 0 and all(c % bn == 0 for c in col0s)
    nj, ni = n // bn, mp // bm
    has_s = xs is not None
    ms = xs.shape[0] if has_s else 0
    n_out = len(out_dtypes)

    in_specs = [pl.BlockSpec((bm, k), lambda j, i: (i, 0))]
    args = [xp]
    if has_s:
        in_specs.append(pl.BlockSpec((ms, k), lambda j, i: (0, 0)))
        args.append(xs)
    for c in col0s:
        if w_t:
            assert w_layer is not None
            in_specs.append(pl.BlockSpec((None, bn, k), lambda j, i, c0=c // bn: (w_layer, j + c0, 0)))
        elif w_layer is None:
            in_specs.append(pl.BlockSpec((k, bn), lambda j, i, c0=c // bn: (0, j + c0)))
        else:
            in_specs.append(pl.BlockSpec((None, k, bn), lambda j, i, c0=c // bn: (w_layer, 0, j + c0)))
        args.append(w)
    for s in shared:
        in_specs.append(pl.BlockSpec(s.shape, lambda j, i: (0, 0)))
        args.append(s)
    for e in extras_p:
        in_specs.append(pl.BlockSpec((bm, bn), lambda j, i: (i, j)))
        args.append(e)
    if has_s:
        for e in extras_s:
            in_specs.append(pl.BlockSpec((ms, bn), lambda j, i: (0, j)))
            args.append(e)
    out_specs = [pl.BlockSpec((bm, bn), lambda j, i: (i, j)) for _ in out_dtypes]
    out_shape = [jax.ShapeDtypeStruct((mp, n), d) for d in out_dtypes]
    if has_s:
        out_specs += [pl.BlockSpec((ms, bn), lambda j, i: (0, j)) for _ in out_dtypes]
        out_shape += [jax.ShapeDtypeStruct((ms, n), d) for d in out_dtypes]
    has_side = side is not None
    if has_side:
        rows, cols = side.shape[-2:]
        slab = rows // (nj * ni)
        assert slab * nj * ni == rows and slab % 16 == 0
        if side_layer is None:
            in_specs.append(pl.BlockSpec((slab, cols), lambda j, i: (j * ni + i, 0)))
        else:
            in_specs.append(pl.BlockSpec((None, slab, cols), lambda j, i: (side_layer, j * ni + i, 0)))
        args.append(side)
        out_specs.append(pl.BlockSpec((slab, cols), lambda j, i: (j * ni + i, 0)))
        out_shape.append(jax.ShapeDtypeStruct((rows, cols), bf16))

    outs = pl.pallas_call(
        functools.partial(_proj_kernel, n_w=len(col0s), n_shared=len(shared), n_extra=len(extras_p),
                          n_out=n_out, has_s=has_s, has_side=has_side, epilogue=epilogue,
                          w_rows_are_outputs=w_t),
        grid=(nj, ni),
        in_specs=in_specs,
        out_specs=out_specs,
        out_shape=out_shape,
        scratch_shapes=[pltpu.VMEM((bn, k) if w_t else (k, bn), bf16) for _ in col0s],
        compiler_params=_cparams(2),
        name=name,
    )(*args)
    return list(outs)


def _layered(w):
    return w if isinstance(w, tuple) else (w, None)


def _side_ok(rows, n, bn, mp, bm):
    steps = (n // bn) * (mp // bm)
    return rows % steps == 0 and (rows // steps) % 16 == 0


def _mm_residual_kernel(x_ref, w_ref, r_ref, o_ref, *, scale):
    acc = _dot(x_ref[...], w_ref[...])
    if scale != 1.0:
        acc = scale * acc
    o_ref[...] = r_ref[...] + acc


def matmul_residual(x, w, res, scale, *, bm=None, bn=None):
    m, k = x.shape
    n = w.shape[1]
    bm = bm or _pick(m, (512, 256, 128))
    bn = bn or _pick(n, (512, 256, 128))
    return pl.pallas_call(
        functools.partial(_mm_residual_kernel, scale=scale),
        grid=(m // bm, n // bn),
        in_specs=[pl.BlockSpec((bm, k), lambda i, j: (i, 0)),
                  pl.BlockSpec((k, bn), lambda i, j: (0, j)),
                  pl.BlockSpec((bm, bn), lambda i, j: (i, j))],
        out_specs=pl.BlockSpec((bm, bn), lambda i, j: (i, j)),
        out_shape=jax.ShapeDtypeStruct((m, n), f32),
        compiler_params=_cparams(2),
        name="matmul_residual",
    )(x, w, res)


def _mm_narrow_kernel(x_ref, w_ref, b_ref, o_ref, *, log_sigmoid):
    acc = _dot_nt(x_ref[...], w_ref[...].astype(bf16))
    o_ref[...] = _log_sigmoid(acc + b_ref[...]) if log_sigmoid else acc


def matmul_narrow(x, w_rows, layer, row0, n, b=None, *, log_sigmoid=False):
    m, k = x.shape
    assert row0 % n == 0
    bm = _pick(m, (1024, 512, 256, 128))
    b = jnp.zeros((n,), f32) if b is None else b
    return pl.pallas_call(
        functools.partial(_mm_narrow_kernel, log_sigmoid=log_sigmoid),
        grid=(m // bm,),
        in_specs=[pl.BlockSpec((bm, k), lambda i: (i, 0)),
                  pl.BlockSpec((None, n, k), lambda i: (layer, row0 // n, 0)),
                  pl.BlockSpec((1, n), lambda i: (0, 0))],
        out_specs=pl.BlockSpec((bm, n), lambda i: (i, 0)),
        out_shape=jax.ShapeDtypeStruct((m, n), f32),
        compiler_params=_cparams(1),
        name="matmul_narrow",
    )(x, w_rows, b.reshape(1, n).astype(f32))


def _mlstm_kernel(q_ref, k_ref, v_ref, o_ref, gates_ref, bias_ref, onorm_ref,
                  c0_ref, n0_ref, m0_ref,
                  y_ref, c_ref, n_ref, m_ref, *, heads, t_valid):
    h = pl.program_id(1)
    ci = pl.program_id(2)
    L, dqk = q_ref.shape

    @pl.when(ci == 0)
    def _():
        c_ref[...] = c0_ref[...]
        n_ref[...] = n0_ref[...]
        m_ref[...] = m0_ref[...]

    row = lax.broadcasted_iota(jnp.int32, (L, L), 0)
    col = lax.broadcasted_iota(jnp.int32, (L, L), 1)
    eye = row == col
    causal = col <= row

    gates = gates_ref[...] + bias_ref[...]
    lane = lax.broadcasted_iota(jnp.int32, gates.shape, 1)
    ig_col = jnp.sum(jnp.where(lane == h, gates, 0.0), axis=1, keepdims=True)
    f_col = jnp.sum(jnp.where(lane == h + heads, gates, 0.0), axis=1, keepdims=True)
    lf_col = _log_sigmoid(f_col)
    if t_valid is not None:
        rvalid = lax.broadcasted_iota(jnp.int32, (L, 1), 0) < t_valid
        lf_col = jnp.where(rvalid, lf_col, 0.0)
        ig_col = jnp.where(rvalid, ig_col, -jnp.inf)

    b_row = jnp.sum(jnp.where(row <= col, lf_col, 0.0), axis=0, keepdims=True)
    lf_row = jnp.sum(jnp.where(eye, lf_col, 0.0), axis=0, keepdims=True)
    b_col = jnp.sum(jnp.where(causal, lf_row, 0.0), axis=1, keepdims=True)
    ig_row = jnp.sum(jnp.where(eye, ig_col, 0.0), axis=0, keepdims=True)
    u_row = ig_row - b_row
    u_col = ig_col - b_col

    m_prev = m_ref[:, 0:1]
    d = jnp.where(causal, b_col + u_row, -jnp.inf)
    inter = b_col + m_prev
    m_t = jnp.maximum(jnp.max(d, axis=1, keepdims=True), inter)
    p = jnp.exp(d - m_t)
    w_inter = jnp.exp(inter - m_t)

    q = q_ref[...].astype(bf16)
    kf = k_ref[...] * (dqk ** -0.5)
    kb = kf.astype(bf16)
    vb = v_ref[...].astype(bf16)
    c_prev = c_ref[...]
    n_prev = n_ref[...]

    s = _dot_nt(q, kb) * p
    num = w_inter * _dot(q, c_prev.astype(bf16)) + _dot(s.astype(bf16), vb)
    qn = jnp.sum(q_ref[...] * n_prev, axis=1, keepdims=True)
    den = w_inter * qn + jnp.sum(s, axis=1, keepdims=True)
    hid = num / jnp.maximum(jnp.abs(den), jnp.exp(-m_t))

    last = lax.broadcasted_iota(jnp.int32, (L, 1), 0) == L - 1
    m_new = jnp.sum(jnp.where(last, m_t, 0.0), axis=0, keepdims=True)
    b_last = jnp.sum(jnp.where(last, b_col, 0.0), axis=0, keepdims=True)
    w_c = jnp.exp(b_last + m_prev - m_new)
    w_s = jnp.exp(b_last + u_col - m_new)
    if t_valid is not None:
        w_s = jnp.where(rvalid, w_s, 0.0)
    kw = kf * w_s
    c_ref[...] = w_c * c_prev + _dot_tn(kw.astype(bf16), vb)
    n_ref[...] = w_c * n_prev + jnp.sum(kw, axis=0, keepdims=True)
    m_ref[...] = jnp.broadcast_to(m_new, m_ref.shape)

    ms = jnp.mean(hid * hid, axis=-1, keepdims=True)
    hn = hid * lax.rsqrt(ms + RMS_EPS) * onorm_ref[...]
    y_ref[...] = (hn * _sigmoid(o_ref[...])).astype(y_ref.dtype)


def mlstm(qkvo, gates, bias, out_norm, c0, n0, m0, *, batch, chunk, t_valid=None):
    bt = qkvo.shape[0]
    t = bt // batch
    heads, dqk, dv = c0.shape[1:]
    nc = t // chunk
    assert t % chunk == 0 and dv % dqk == 0
    kq = heads
    kv = 2 * heads * dqk // dv
    ko = kv + heads
    n0r = n0.reshape(batch * heads, 1, dqk).astype(f32)
    m0r = jnp.broadcast_to(m0.reshape(batch * heads, 1, 1).astype(f32), (batch * heads, 1, LANES))
    y, c, n, m = pl.pallas_call(
        functools.partial(_mlstm_kernel, heads=heads, t_valid=t_valid),
        grid=(batch, heads, nc),
        in_specs=[
            pl.BlockSpec((chunk, dqk), lambda b, h, c: (b * nc + c, h)),
            pl.BlockSpec((chunk, dqk), lambda b, h, c: (b * nc + c, kq + h)),
            pl.BlockSpec((chunk, dv), lambda b, h, c: (b * nc + c, kv + h)),
            pl.BlockSpec((chunk, dv), lambda b, h, c: (b * nc + c, ko + h)),
            pl.BlockSpec((chunk, 2 * heads), lambda b, h, c: (b * nc + c, 0)),
            pl.BlockSpec((1, 2 * heads), lambda b, h, c: (0, 0)),
            pl.BlockSpec((None, 1, dv), lambda b, h, c: (h, 0, 0)),
            pl.BlockSpec((None, None, dqk, dv), lambda b, h, c: (b, h, 0, 0)),
            pl.BlockSpec((None, 1, dqk), lambda b, h, c: (b * heads + h, 0, 0)),
            pl.BlockSpec((None, 1, LANES), lambda b, h, c: (b * heads + h, 0, 0)),
        ],
        out_specs=[
            pl.BlockSpec((chunk, dv), lambda b, h, c: (b * nc + c, h)),
            pl.BlockSpec((None, None, dqk, dv), lambda b, h, c: (b, h, 0, 0)),
            pl.BlockSpec((None, 1, dqk), lambda b, h, c: (b * heads + h, 0, 0)),
            pl.BlockSpec((None, 1, LANES), lambda b, h, c: (b * heads + h, 0, 0)),
        ],
        out_shape=[
            jax.ShapeDtypeStruct((bt, heads * dv), bf16),
            jax.ShapeDtypeStruct((batch, heads, dqk, dv), f32),
            jax.ShapeDtypeStruct((batch * heads, 1, dqk), f32),
            jax.ShapeDtypeStruct((batch * heads, 1, LANES), f32),
        ],
        compiler_params=_cparams(3),
        name="mlstm",
    )(qkvo, qkvo, qkvo, qkvo, gates, bias.reshape(1, 2 * heads).astype(f32),
      out_norm.reshape(heads, 1, dv).astype(f32), c0.astype(f32), n0r, m0r)
    return y, c, n.reshape(batch, heads, dqk), m[:, 0, 0].reshape(batch, heads)


def _lane_cumsum(x, reverse=False):
    lane = lax.broadcasted_iota(jnp.int32, x.shape, x.ndim - 1)
    sh = 1
    while sh < LANES:
        if reverse:
            x = x + jnp.where(lane + sh < LANES, pltpu.roll(x, LANES - sh, x.ndim - 1), 0.0)
        else:
            x = x + jnp.where(lane >= sh, pltpu.roll(x, sh, x.ndim - 1), 0.0)
        sh *= 2
    return x


def _cumsum_kernel(x_ref, o_ref):
    hh, t = x_ref.shape
    carry = jnp.zeros((hh, 1), f32)
    for c in range(t // LANES):
        y = _lane_cumsum(x_ref[:, c * LANES:(c + 1) * LANES]) + carry
        o_ref[:, c * LANES:(c + 1) * LANES] = y * LOG2E
        carry = y[:, LANES - 1:LANES]


def cumsum_tokens_log2(x):
    b, hh, t = x.shape
    return pl.pallas_call(
        _cumsum_kernel,
        grid=(b,),
        in_specs=[pl.BlockSpec((None, hh, t), lambda i: (i, 0, 0))],
        out_specs=pl.BlockSpec((None, hh, t), lambda i: (i, 0, 0)),
        out_shape=jax.ShapeDtypeStruct((b, hh, t), f32),
        compiler_params=_cparams(1),
        name="cumsum_tokens",
    )(x)


def _fox_prompt_kernel(q_ref, k_ref, v_ref, c_ref, o_ref, *, blk, hd, n_heads):
    qi = pl.program_id(2)

    def attend(blocks, carry):
        out = []
        for hh in range(n_heads):
            m, l, acc = carry[hh]
            cols = slice(hh * hd, (hh + 1) * hd)
            q = q_ref[:, cols]
            scores = []
            for j, mask in blocks:
                start = pl.multiple_of(j * blk, blk)
                s = _dot_nt(q, k_ref[pl.ds(start, blk), cols]) - c_ref[hh, j]
                if mask:
                    row = lax.broadcasted_iota(jnp.int32, (blk, blk), 0)
                    col = lax.broadcasted_iota(jnp.int32, (blk, blk), 1)
                    s = jnp.where(col <= row, s, -jnp.inf)
                scores.append(s)
            m_new = m
            for s in scores:
                m_new = jnp.maximum(m_new, jnp.max(s, axis=1, keepdims=True))
            alpha = jnp.exp2(m - m_new)
            l = alpha * l
            acc = alpha * acc
            for (j, _), s in zip(blocks, scores):
                start = pl.multiple_of(j * blk, blk)
                p = jnp.exp2(s - m_new)
                l = l + jnp.sum(p, axis=1, keepdims=True)
                acc = acc + _dot(p.astype(bf16), v_ref[pl.ds(start, blk), cols])
            out.append((m_new, l, acc))
        return tuple(out)

    init = tuple((jnp.full((blk, 1), -jnp.inf, f32), jnp.zeros((blk, 1), f32), jnp.zeros((blk, hd), f32))
                 for _ in range(n_heads))
    carry = lax.fori_loop(0, qi // 2, lambda j, c: attend([(2 * j, False), (2 * j + 1, False)], c), init)
    carry = lax.cond(qi % 2 == 1,
                     lambda c: attend([(qi - 1, False), (qi, True)], c),
                     lambda c: attend([(qi, True)], c), carry)
    for hh in range(n_heads):
        _, l, acc = carry[hh]
        o_ref[:, hh * hd:(hh + 1) * hd] = (acc / l).astype(o_ref.dtype)


def fox_prompt_attention(q, k, v, c, *, batch, heads):
    bt, width = q.shape
    t = bt // batch
    hd = width // heads
    blk = _pick(t, (FOX_Q_BLOCK, LANES))
    nq = t // blk
    nh = FOX_HEADS_PER_STEP if heads % FOX_HEADS_PER_STEP == 0 else 1
    c5 = c.reshape(batch, heads, nq, 1, blk)
    return pl.pallas_call(
        functools.partial(_fox_prompt_kernel, blk=blk, hd=hd, n_heads=nh),
        grid=(batch, heads // nh, nq),
        in_specs=[
            pl.BlockSpec((blk, nh * hd), lambda b, h, i: (b * nq + i, h)),
            pl.BlockSpec((t, nh * hd), lambda b, h, i: (b, h)),
            pl.BlockSpec((t, nh * hd), lambda b, h, i: (b, h)),
            pl.BlockSpec((None, nh, nq, 1, blk), lambda b, h, i: (b, h, 0, 0, 0)),
        ],
        out_specs=pl.BlockSpec((blk, nh * hd), lambda b, h, i: (b * nq + i, h)),
        out_shape=jax.ShapeDtypeStruct((bt, width), bf16),
        compiler_params=_cparams(3),
        name="fox_prompt_attention",
    )(q, k, v, c5)


def _fox_decay_kernel(pt_ref, *refs, n_pages):
    lf_refs = refs[:n_pages]
    r_ref, carry_ref = refs[n_pages:]
    hh, page = lf_refs[0].shape

    @pl.when(pl.program_id(1) == 0)
    def _():
        carry_ref[...] = jnp.zeros_like(carry_ref)

    x_all = jnp.concatenate([lf_refs[g][...] for g in range(n_pages)], axis=0)
    y_all = _lane_cumsum(x_all, reverse=True)
    carry = carry_ref[...]
    for g in reversed(range(n_pages)):
        x = x_all[g * hh:(g + 1) * hh]
        y = y_all[g * hh:(g + 1) * hh]
        r_ref[:, g * page:(g + 1) * page] = ((y - x) + carry) * LOG2E
        carry = carry + y[:, 0:1]
    carry_ref[...] = carry


def fox_past_decay_log2(pool_lf_t, layer, page_table):
    _, _, hh, page = pool_lf_t.shape
    b, npg = page_table.shape
    g = _pick(npg, (DECAY_PAGES_PER_STEP, 8, 4, 2, 1))
    ns = npg // g
    assert page == LANES

    def lf_spec(gi):
        return pl.BlockSpec((None, None, hh, page),
                            lambda i, s, pt: (layer, pt[i, (ns - 1 - s) * g + gi], 0, 0))

    return pl.pallas_call(
        functools.partial(_fox_decay_kernel, n_pages=g),
        grid_spec=pltpu.PrefetchScalarGridSpec(
            num_scalar_prefetch=1,
            grid=(b, ns),
            in_specs=[lf_spec(gi) for gi in range(g)],
            out_specs=pl.BlockSpec((None, hh, g * page), lambda i, s, pt: (i, 0, ns - 1 - s)),
            scratch_shapes=[pltpu.VMEM((hh, page), f32)],
        ),
        out_shape=jax.ShapeDtypeStruct((b, hh, npg * page), f32),
        compiler_params=_cparams(2),
        name="fox_past_decay",
    )(page_table, *([pool_lf_t] * g))


def _fox_decode_kernel(pt_ref, q_ref, *refs, n_pages, t_valid, ht):
    kps = refs[:n_pages]
    vps = refs[n_pages:2 * n_pages]
    r_ref, kn_ref, vn_ref, a_ref, o_ref, m_ref, l_ref, acc_ref = refs[2 * n_pages:]
    p = pl.program_id(1)
    last = pl.num_programs(1) - 1
    tq, heads, hd = q_ref.shape
    nt = heads // ht
    rq = tq * ht
    page = kps[0].shape[0]
    rows_n = kn_ref.shape[0]

    @pl.when(p == 0)
    def _():
        m_ref[...] = jnp.full_like(m_ref, -jnp.inf)
        l_ref[...] = jnp.zeros_like(l_ref)
        acc_ref[...] = jnp.zeros_like(acc_ref)

    def tile(ref, a):
        x = ref[:, a * ht:(a + 1) * ht, :]
        return x.reshape(x.shape[0] * ht, hd).astype(bf16)

    def attend(a, qa, blocks):
        scores = [_dot_nt(qa, kb) + bias for kb, _, bias in blocks]
        m_old = m_ref[a]
        m_new = m_old
        for s in scores:
            m_new = jnp.maximum(m_new, jnp.max(s, axis=1, keepdims=True))
        alpha = jnp.exp2(m_old - m_new)
        l = alpha * l_ref[a]
        acc = alpha * acc_ref[a]
        for (_, vb, _), s in zip(blocks, scores):
            pr = jnp.exp2(s - m_new)
            l = l + jnp.sum(pr, axis=1, keepdims=True)
            acc = acc + _dot(pr.astype(bf16), vb)
        l_ref[a] = l
        acc_ref[a] = acc
        m_ref[a] = m_new

    row = lax.broadcasted_iota(jnp.int32, (rq, page * ht), 0)
    col = lax.broadcasted_iota(jnp.int32, (rq, page * ht), 1)
    head_mask = jnp.where((col & (ht - 1)) == (row & (ht - 1)), 0.0, -jnp.inf)
    for a in range(nt):
        attend(a, tile(q_ref, a),
               [(tile(kps[g], a), tile(vps[g], a), r_ref[g, a] + head_mask) for g in range(n_pages)])

    @pl.when(p == last)
    def _():
        shift = ht.bit_length() - 1
        rown = lax.broadcasted_iota(jnp.int32, (rq, rows_n * ht), 0)
        coln = lax.broadcasted_iota(jnp.int32, (rq, rows_n * ht), 1)
        tk = coln >> shift
        allowed = ((coln & (ht - 1)) == (rown & (ht - 1))) & (tk <= (rown >> shift)) & (tk < t_valid)
        for a in range(nt):
            bias = jnp.where(allowed, -a_ref[a], -jnp.inf)
            attend(a, tile(q_ref, a), [(tile(kn_ref, a), tile(vn_ref, a), bias)])
            o_ref[a] = acc_ref[a] / l_ref[a]


def fox_decode_attention(q, pool_k, pool_v, layer, page_table, r, k_new, v_new, a_new, *, t_valid):
    b, tq, heads, hd = q.shape
    page = pool_k.shape[2]
    npg = page_table.shape[1]
    rows_n = k_new.shape[1]
    ht = SUBLANES
    assert heads % ht == 0
    nt = heads // ht
    g = _pick(npg, (DECODE_PAGES_PER_STEP, 1))

    def pool_spec(gi):
        return pl.BlockSpec((None, None, page, heads, hd), lambda i, p, pt: (layer, pt[i, p * g + gi], 0, 0, 0))

    seq4 = lambda i, p, pt: (i, 0, 0, 0)
    return pl.pallas_call(
        functools.partial(_fox_decode_kernel, n_pages=g, t_valid=t_valid, ht=ht),
        grid_spec=pltpu.PrefetchScalarGridSpec(
            num_scalar_prefetch=1,
            grid=(b, npg // g),
            in_specs=[pl.BlockSpec((None, tq, heads, hd), seq4)]
            + [pool_spec(gi) for gi in range(g)] + [pool_spec(gi) for gi in range(g)]
            + [pl.BlockSpec((None, g, nt, 1, page * ht), lambda i, p, pt: (i, p, 0, 0, 0)),
               pl.BlockSpec((None, rows_n, heads, hd), seq4),
               pl.BlockSpec((None, rows_n, heads, hd), seq4),
               pl.BlockSpec((None, nt, 1, rows_n * ht), seq4)],
            out_specs=pl.BlockSpec((None, nt, tq * ht, hd), seq4),
            scratch_shapes=[pltpu.VMEM((nt, tq * ht, 1), f32), pltpu.VMEM((nt, tq * ht, 1), f32),
                            pltpu.VMEM((nt, tq * ht, hd), f32)],
        ),
        out_shape=jax.ShapeDtypeStruct((b, nt, tq * ht, hd), f32),
        compiler_params=_cparams(2),
        name="fox_decode_attention",
    )(page_table, q, *([pool_k] * g), *([pool_v] * g), r, k_new, v_new, a_new)


def _xattn_kernel(q_ref, k_ref, v_ref, o_ref, *, head_dim):
    for s in range(0, q_ref.shape[1], head_dim):
        q = q_ref[:, s:s + head_dim]
        kb = k_ref[:, s:s + head_dim].astype(bf16)
        vb = v_ref[:, s:s + head_dim].astype(bf16)
        sc = _dot_nt(q, kb)
        m = jnp.max(sc, axis=1, keepdims=True)
        p = jnp.exp(sc - m)
        l = jnp.sum(p, axis=1, keepdims=True)
        o_ref[:, s:s + head_dim] = (_dot(p.astype(bf16), vb) / l).astype(o_ref.dtype)


def mem_cross_attention(q, mk, mv, *, head_dim):
    b, t, w = q.shape
    mt = mk.shape[1]
    bq = _pick(t, (512, 256, 128))
    return pl.pallas_call(
        functools.partial(_xattn_kernel, head_dim=head_dim),
        grid=(b, t // bq),
        in_specs=[pl.BlockSpec((None, bq, w), lambda i, j: (i, j, 0)),
                  pl.BlockSpec((None, mt, w), lambda i, j: (i, 0, 0)),
                  pl.BlockSpec((None, mt, w), lambda i, j: (i, 0, 0))],
        out_specs=pl.BlockSpec((None, bq, w), lambda i, j: (i, j, 0)),
        out_shape=jax.ShapeDtypeStruct((b, t, w), bf16),
        compiler_params=_cparams(2),
        name="mem_cross_attention",
    )(q, mk, mv)


def _ffn(yp, ys, norm, w_in, w_out, layer):
    f = w_in.shape[-1] // 2
    bn = _pick(f, (256, 128))
    mp = yp.shape[0]
    bm = _pick(mp, (1024, 512, 256, 128))
    side = (w_out, layer) if _side_ok(w_out.shape[-2], f, bn, mp, bm) else None
    outs = _project(rmsnorm(yp, norm), rmsnorm(ys, norm), (w_in, layer), col0s=(0, f), n=f, bn=bn, bm=bm,
                    epilogue=_epi_swiglu, out_dtypes=(bf16,), side=side, name="ffn_in")
    hp, hs = outs[0], outs[1]
    wo = outs[2] if side is not None else w_out[layer].astype(bf16)
    return (matmul_residual(hp, wo, yp, FFN_RES_SCALE), matmul_residual(hs, wo, ys, FFN_RES_SCALE))


def _out_proj(xp, xs, w, yp, ys, name):
    n = _layered(w)[0].shape[-1]
    return _project(xp, xs, w, col0s=(0,), n=n, bn=_pick(n, (512, 256, 128)),
                    epilogue=functools.partial(_epi_residual, scale=1.0), out_dtypes=(f32,),
                    extras_p=(yp,), extras_s=(ys,), name=name)


def _row(v):
    return v.reshape(1, -1).astype(f32)


def kernel(x_prompt, x_sample, state_mlstm_C, state_mlstm_n, state_mlstm_m, cache_fox_k, cache_fox_v, cache_fox_logf, cache_mem_k, cache_mem_v, page_table, mem_prompt, norm_ffn1, w_ffn1_in, w_ffn1_out, norm_mix, w_mlstm_in, b_mlstm_gate, mlstm_out_norm, w_mlstm_out, w_fox_in, b_fox_f, fox_q_norm, fox_k_norm, w_fox_out, norm_xattn, norm_mem, w_xq, w_xkv, xq_norm, xk_norm, w_xo, norm_ffn2, w_ffn2_in, w_ffn2_out):
    bp, seq, d = x_prompt.shape
    bs, dec_seq, _ = x_sample.shape
    depth = norm_ffn1.shape[0]
    n_mixers = 2
    m_heads = b_mlstm_gate.shape[-1] // 2
    m_dv = mlstm_out_norm.shape[-1]
    m_dqk = state_mlstm_n.shape[-1]
    m_main = 2 * m_heads * (m_dqk + m_dv)
    f_heads = b_fox_f.shape[-1]
    f_hd = fox_q_norm.shape[-1]
    f_width = f_heads * f_hd
    page = cache_fox_k.shape[2]
    x_heads, x_hd = cache_mem_k.shape[3], cache_mem_k.shape[4]
    x_width = x_heads * x_hd
    mem_tokens = mem_prompt.shape[1]
    rows = SAMPLE_ROWS
    ht = SUBLANES
    assert dec_seq <= rows and page == LANES

    yp = x_prompt.reshape(bp * seq, d)
    ys = jnp.pad(x_sample, ((0, 0), (0, rows - dec_seq), (0, 0))).reshape(bs * rows, d)
    memp = mem_prompt.reshape(bp * mem_tokens, d)
    chunk_p = _pick(seq, (MLSTM_PROMPT_CHUNK, 128, 64))

    c_p, n_p, m_p, c_s, n_s, m_s = [], [], [], [], [], []
    fk_p, fv_p, fl_p, fk_s, fv_s, fl_s = [], [], [], [], [], []
    mk_p, mv_p = [], []
    for layer in range(depth):
        j = layer // n_mixers
        yp, ys = _ffn(yp, ys, norm_ffn1[layer], w_ffn1_in, w_ffn1_out, layer)

        hp, hs = rmsnorm(yp, norm_mix[layer]), rmsnorm(ys, norm_mix[layer])
        if layer % n_mixers == 0:
            w_rows = jnp.swapaxes(w_mlstm_in, 1, 2)
            qkvo_p, qkvo_s = _project(hp, hs, (w_rows, j), col0s=(0,), n=m_main, bn=_pick(m_main, (512, 256, 128)),
                                      epilogue=_epi_plain, out_dtypes=(f32,), w_t=True, name="mlstm_in")
            n_gate = 2 * m_heads
            zc = jnp.zeros((bp, m_heads, m_dqk, m_dv), f32)
            zn = jnp.zeros((bp, m_heads, m_dqk), f32)
            zm = jnp.zeros((bp, m_heads), f32)
            gp, c1, n1, m1 = mlstm(qkvo_p, matmul_narrow(hp, w_rows, j, m_main, n_gate), b_mlstm_gate[j],
                                   mlstm_out_norm[j],
                                   zc, zn, zm, batch=bp, chunk=chunk_p)
            gs, c2, n2, m2 = mlstm(qkvo_s, matmul_narrow(hs, w_rows, j, m_main, n_gate), b_mlstm_gate[j],
                                   mlstm_out_norm[j],
                                   state_mlstm_C[j], state_mlstm_n[j], state_mlstm_m[j],
                                   batch=bs, chunk=rows, t_valid=dec_seq)
            yp, ys = _out_proj(gp, gs, (w_mlstm_out, j), yp, ys, "mlstm_out")
            c_p.append(c1); n_p.append(n1); m_p.append(m1)
            c_s.append(c2); n_s.append(n2); m_s.append(m2)
        else:
            w_rows = jnp.swapaxes(w_fox_in, 1, 2)
            w_in = (w_rows, j)
            bn = _pick(f_width, (512, 256, 128))
            q_p, q_s = _project(hp, hs, w_in, col0s=(0,), n=f_width, bn=bn, w_t=True, name="fox_q",
                                epilogue=functools.partial(_epi_headnorm, scale=f_hd ** -0.5 * LOG2E, head_dim=f_hd),
                                out_dtypes=(bf16,), shared=(_row(fox_q_norm[j]),))
            k_p, kb_p, k_s, _ = _project(hp, hs, w_in, col0s=(f_width,), n=f_width, bn=bn, w_t=True, name="fox_k",
                                         epilogue=functools.partial(_epi_headnorm, scale=1.0, head_dim=f_hd),
                                         out_dtypes=(f32, bf16), shared=(_row(fox_k_norm[j]),))
            v_p, vb_p, v_s, _ = _project(hp, hs, w_in, col0s=(2 * f_width,), n=f_width, bn=bn, w_t=True,
                                         name="fox_v", epilogue=_epi_plain, out_dtypes=(f32, bf16))
            lf_p = matmul_narrow(hp, w_rows, j, 3 * f_width, f_heads, b_fox_f[j], log_sigmoid=True)
            lf_s = matmul_narrow(hs, w_rows, j, 3 * f_width, f_heads, b_fox_f[j], log_sigmoid=True)
            c = cumsum_tokens_log2(lf_p.reshape(bp, seq, f_heads).transpose(0, 2, 1))
            o_p = fox_prompt_attention(q_p, kb_p, vb_p, c, batch=bp, heads=f_heads)
            fk_p.append(k_p.reshape(bp, seq // page, page, f_heads, f_hd))
            fv_p.append(v_p.reshape(bp, seq // page, page, f_heads, f_hd))
            fl_p.append(lf_p.reshape(bp, seq // page, page, f_heads))
            nt = f_heads // ht
            npg = page_table.shape[1]
            lf_t = lf_s.reshape(bs, rows, f_heads).transpose(0, 2, 1)
            a = cumsum_tokens_log2(jnp.pad(lf_t, ((0, 0), (0, 0), (0, LANES - rows))))[:, :, :rows]
            a = a.reshape(bs, nt, ht, rows).transpose(0, 1, 3, 2).reshape(bs, nt, 1, rows * ht)
            r = fox_past_decay_log2(jnp.swapaxes(cache_fox_logf, 2, 3), j, page_table)
            r = r.reshape(bs, nt, ht, npg, page).transpose(0, 3, 1, 4, 2).reshape(bs, npg, nt, 1, page * ht)
            q4 = q_s.reshape(bs, rows, f_heads, f_hd)[:, :dec_seq].astype(f32)
            k4 = k_s.reshape(bs, rows, f_heads, f_hd)
            v4 = v_s.reshape(bs, rows, f_heads, f_hd)
            o = fox_decode_attention(q4, cache_fox_k, cache_fox_v, j, page_table, r, k4, v4, a, t_valid=dec_seq)
            o = o.reshape(bs, nt, dec_seq, ht, f_hd).transpose(0, 2, 1, 3, 4).reshape(bs, dec_seq, f_width)
            o_s = jnp.pad(o, ((0, 0), (0, rows - dec_seq), (0, 0))).reshape(bs * rows, f_width).astype(bf16)
            yp, ys = _out_proj(o_p, o_s, (w_fox_out, j), yp, ys, "fox_out")
            fk_s.append(k4[:, :dec_seq])
            fv_s.append(v4[:, :dec_seq])
            fl_s.append(lf_s.reshape(bs, rows, f_heads)[:, :dec_seq])

        mem_n = rmsnorm(memp, norm_mem[layer])
        mk, = _project(mem_n, None, (w_xkv, layer), col0s=(0,), n=x_width, bn=x_width, name="mem_k",
                       epilogue=functools.partial(_epi_headnorm, scale=1.0, head_dim=x_hd),
                       out_dtypes=(f32,), shared=(_row(xk_norm[layer]),))
        mv, = _project(mem_n, None, (w_xkv, layer), col0s=(x_width,), n=x_width, bn=x_width, name="mem_v",
                       epilogue=_epi_plain, out_dtypes=(f32,))
        mk_p.append(mk.reshape(bp, mem_tokens, x_heads, x_hd))
        mv_p.append(mv.reshape(bp, mem_tokens, x_heads, x_hd))
        xq_p, xq_s = _project(rmsnorm(yp, norm_xattn[layer]), rmsnorm(ys, norm_xattn[layer]), (w_xq, layer),
                              col0s=(0,), n=x_width, bn=x_width, name="xattn_q",
                              epilogue=functools.partial(_epi_headnorm, scale=x_hd ** -0.5, head_dim=x_hd),
                              out_dtypes=(bf16,), shared=(_row(xq_norm[layer]),))
        xo_p = mem_cross_attention(xq_p.reshape(bp, seq, x_width), mk.reshape(bp, mem_tokens, x_width),
                                   mv.reshape(bp, mem_tokens, x_width), head_dim=x_hd)
        xo_s = mem_cross_attention(xq_s.reshape(bs, rows, x_width),
                                   cache_mem_k[layer].reshape(bs, mem_tokens, x_width),
                                   cache_mem_v[layer].reshape(bs, mem_tokens, x_width), head_dim=x_hd)
        yp, ys = _out_proj(xo_p.reshape(bp * seq, x_width), xo_s.reshape(bs * rows, x_width), (w_xo, layer),
                           yp, ys, "xattn_out")

        yp, ys = _ffn(yp, ys, norm_ffn2[layer], w_ffn2_in, w_ffn2_out, layer)

    yp = yp.reshape(bp, seq, d)
    ys = ys.reshape(bs, rows, d)[:, :dec_seq]
    return (yp, ys, jnp.stack(c_p), jnp.stack(n_p), jnp.stack(m_p), jnp.stack(fk_p), jnp.stack(fv_p),
            jnp.stack(fl_p), jnp.stack(mk_p), jnp.stack(mv_p), jnp.stack(c_s), jnp.stack(n_s), jnp.stack(m_s),
            jnp.stack(fk_s), jnp.stack(fv_s), jnp.stack(fl_s))
```
